```python
import math
import jax, jax.numpy as jnp
from jax import lax
import numpy as np

D_MODEL = 1024
BATCH = 2
SEQ = 8192
DEPTH = 4
DEC_BATCH = 128
DEC_SEQ = 8
PAST_LEN = 2048
PAGE_SIZE = 128

F32 = jnp.float32
N_EVEN = (DEPTH + 1) // 2
N_ODD = DEPTH // 2
N_HEADS = 8
N_KV = 2
GQ = N_HEADS // N_KV
HEAD_DIM = 64
ROPE_DIM = HEAD_DIM // 4
ROPE_THETA = 500000.0
CMP_BLOCK = 32
CMP_STRIDE = 16
CMP_RATIO = CMP_BLOCK // CMP_STRIDE
SLC_BLOCK = 64
N_SELECT = 16
WINDOW = 512
Q_BLOCK = 128
SGU_GROUPS = 4
SGU_DIM = 128
SGU_CHUNK = 128
Q_W = N_HEADS * HEAD_DIM
KV_W = 6 * N_KV * HEAD_DIM
GATE_W = 3 * N_HEADS
U_W = SGU_GROUPS * SGU_DIM
IN_COLS = Q_W + KV_W + GATE_W + 2 * U_W
MIX_W = Q_W + U_W
S5_GROUP = 16
S5_GROUPS = D_MODEL // S5_GROUP
S5_STATE = 64
S5_CHUNK = 128
D_FF = -(-8 * D_MODEL // (3 * 256)) * 256
EPS = 1e-6
NEG_INF = -1e30
TINY = 1e-30
FORCE_SCORE = 1e4

kernel_name = 'hybrid_nsa_sgu_s5_decoder_step'


def rmsnorm(x, g):
    xf = x.astype(F32)
    y = xf * lax.rsqrt(jnp.mean(xf * xf, axis=-1, keepdims=True) + EPS)
    return (y * g.astype(F32)).astype(x.dtype)


def layernorm(x, g, b):
    xf = x.astype(F32)
    xc = xf - jnp.mean(xf, axis=-1, keepdims=True)
    var = jnp.mean(xc * xc, axis=-1, keepdims=True)
    return (xc * lax.rsqrt(var + EPS) * g.astype(F32) + b.astype(F32)).astype(x.dtype)


def rope(x, pos):
    half = ROPE_DIM // 2
    inv_freq = ROPE_THETA ** (-jnp.arange(half, dtype=F32) * (2.0 / ROPE_DIM))
    ang = pos.astype(F32)[:, None] * inv_freq[None, :]
    shape = (ang.shape[0],) + (1,) * (x.ndim - 3) + (half,)
    cos = jnp.cos(ang).reshape(shape)
    sin = jnp.sin(ang).reshape(shape)
    xf = x.astype(F32)
    x1 = xf[..., :half]
    x2 = xf[..., half:ROPE_DIM]
    out = jnp.concatenate([x1 * cos - x2 * sin, x2 * cos + x1 * sin, xf[..., ROPE_DIM:]], axis=-1)
    return out.astype(x.dtype)


def masked_softmax(s, mask):
    s = jnp.where(mask, s.astype(F32), NEG_INF)
    m = jnp.max(s, axis=-1, keepdims=True)
    e = jnp.where(mask, jnp.exp(s - m), 0.0)
    return e / jnp.maximum(jnp.sum(e, axis=-1, keepdims=True), TINY)


def project_even(h, pos, w_in):
    N, T, _ = h.shape
    p = h @ w_in
    off = 0
    q = p[..., off:off + Q_W].reshape(N, T, N_HEADS, HEAD_DIM)
    off += Q_W
    kv = p[..., off:off + KV_W].reshape(N, T, 3, 2, N_KV, HEAD_DIM)
    off += KV_W
    gate = jax.nn.sigmoid(p[..., off:off + GATE_W].astype(F32)).reshape(N, T, N_HEADS, 3)
    off += GATE_W
    u = jax.nn.gelu(p[..., off:off + U_W]).reshape(N, T, SGU_GROUPS, SGU_DIM)
    off += U_W
    v = jax.nn.gelu(p[..., off:off + U_W]).reshape(N, T, SGU_GROUPS, SGU_DIM)
    q = rope(q, pos)
    k = rope(kv[:, :, :, 0], pos)
    kv6 = jnp.stack([k, kv[:, :, :, 1]], axis=3).reshape(N, T, 6, N_KV, HEAD_DIM)
    return q, gate, kv6, u, v


def compress(kvc, pe, w1, w2):
    N, T = kvc.shape[:2]
    n_ch = T // CMP_STRIDE
    n_cmp = n_ch - CMP_RATIO + 1
    ch = kvc[:, :n_ch * CMP_STRIDE].reshape(N, n_ch, CMP_STRIDE, 2, N_KV, HEAD_DIM)
    pe_r = pe.reshape(2, CMP_RATIO, CMP_STRIDE, HEAD_DIM)
    w1_r = w1.reshape(2, CMP_RATIO, CMP_STRIDE, HEAD_DIM, HEAD_DIM)
    parts = []
    for m in range(CMP_RATIO):
        seg = ch[:, m:m + n_cmp] + jnp.transpose(pe_r[:, m], (1, 0, 2))[:, :, None, :]
        parts.append(jnp.einsum('nisckd,csde->nicke', seg, w1_r[:, m]))
    hid = sum(parts)
    return jnp.einsum('nicke,cef->nickf', jax.nn.gelu(hid), w2)


def slc_blocks(kvs):
    N, T = kvs.shape[:2]
    n_slc = -(-T // SLC_BLOCK)
    kvs = jnp.pad(kvs, ((0, 0), (0, n_slc * SLC_BLOCK - T), (0, 0), (0, 0), (0, 0)))
    kvs = kvs.reshape(N, n_slc, SLC_BLOCK, 2, N_KV, HEAD_DIM)
    return jnp.transpose(kvs, (0, 4, 1, 2, 3, 5))


def nsa_attend(q, gate, q_pos, kv_cmp, kvs_t, kv_win, win_pos):
    N, Tq = q.shape[:2]
    n_cmp = kv_cmp.shape[1]
    n_slc = kvs_t.shape[2]
    scale = HEAD_DIM ** -0.5
    qg = q.reshape(N, Tq, N_KV, GQ, HEAD_DIM)
    tq = q_pos[None, :, None, None, None]
    cmp_end = jnp.arange(n_cmp) * CMP_STRIDE + (CMP_BLOCK - 1)
    s_c = jnp.einsum('nqkgd,nikd->nqkgi', qg, kv_cmp[:, :, 0]) * scale
    p_c = masked_softmax(s_c, cmp_end <= tq)
    o_c = jnp.einsum('nqkgi,nikd->nqkgd', p_c.astype(q.dtype), kv_cmp[:, :, 1])
    ci = jnp.arange(n_cmp)[:, None]
    sj = jnp.arange(n_slc)[None, :]
    cover = ((ci * CMP_STRIDE < (sj + 1) * SLC_BLOCK) & (ci * CMP_STRIDE + CMP_BLOCK > sj * SLC_BLOCK)).astype(F32)
    imp = jnp.einsum('nqkgi,ij->nqkj', p_c, cover)
    blk = jnp.arange(n_slc)
    cur = (q_pos // SLC_BLOCK)[:, None]
    forced = (blk == 0) | (blk == cur) | (blk == cur - 1)
    allowed = blk * SLC_BLOCK <= q_pos[:, None]
    score = jnp.where(forced[None, :, None], FORCE_SCORE, jnp.where(allowed[None, :, None], imp, -1.0))
    n_sel = min(N_SELECT, n_slc)
    _, idx = lax.top_k(score, n_sel)
    sel = kvs_t[jnp.arange(N)[:, None, None, None], jnp.arange(N_KV)[None, None, :, None], idx]
    sel = sel.reshape(N, Tq, N_KV, n_sel * SLC_BLOCK, 2, HEAD_DIM)
    sel_pos = (idx[..., None] * SLC_BLOCK + jnp.arange(SLC_BLOCK)).reshape(N, Tq, N_KV, 1, n_sel * SLC_BLOCK)
    s_s = jnp.einsum('nqkgd,nqkmd->nqkgm', qg, sel[..., 0, :]) * scale
    p_s = masked_softmax(s_s, sel_pos <= tq)
    o_s = jnp.einsum('nqkgm,nqkmd->nqkgd', p_s.astype(q.dtype), sel[..., 1, :])
    s_w = jnp.einsum('nqkgd,nwkd->nqkgw', qg, kv_win[:, :, 0]) * scale
    diff = q_pos[:, None] - win_pos[None, :]
    m_w = (diff >= 0) & (diff < WINDOW) & (win_pos >= 0)[None, :]
    p_w = masked_softmax(s_w, m_w[None, :, None, None, :])
    o_w = jnp.einsum('nqkgw,nwkd->nqkgd', p_w.astype(q.dtype), kv_win[:, :, 1])
    g = gate.reshape(N, Tq, N_KV, GQ, 3)
    o = g[..., 0:1] * o_c + g[..., 1:2] * o_s + g[..., 2:3] * o_w
    return o.astype(q.dtype).reshape(N, Tq, Q_W)


def nsa_prompt(q, gate, kv6, pe, w1, w2):
    N, T = q.shape[:2]
    kv_cmp = compress(kv6[:, :, 0:2], pe, w1, w2)
    kvs_t = slc_blocks(kv6[:, :, 2:4])
    kw_pad = jnp.pad(kv6[:, :, 4:6], ((0, 0), (WINDOW, 0), (0, 0), (0, 0), (0, 0)))

    def one_block(s0):
        q_b = lax.dynamic_slice_in_dim(q, s0, Q_BLOCK, axis=1)
        g_b = lax.dynamic_slice_in_dim(gate, s0, Q_BLOCK, axis=1)
        kw_b = lax.dynamic_slice_in_dim(kw_pad, s0, WINDOW + Q_BLOCK, axis=1)
        q_pos = s0 + jnp.arange(Q_BLOCK)
        w_pos = s0 - WINDOW + jnp.arange(WINDOW + Q_BLOCK)
        return nsa_attend(q_b, g_b, q_pos, kv_cmp, kvs_t, kw_b, w_pos)

    o = lax.map(one_block, jnp.arange(T // Q_BLOCK) * Q_BLOCK)
    return jnp.moveaxis(o, 0, 1).reshape(N, T, Q_W)


def nsa_sample(q, gate, kv6, cache_kv, win_buf, page_table, pe, w1, w2):
    N, Tn = q.shape[:2]
    past_len = page_table.shape[1] * cache_kv.shape[1]
    past = cache_kv[page_table].reshape(N, past_len, 4, N_KV, HEAD_DIM)
    full = jnp.concatenate([past, kv6[:, :, 0:4]], axis=1)
    kv_cmp = compress(full[:, :, 0:2], pe, w1, w2)
    kvs_t = slc_blocks(full[:, :, 2:4])
    wb = win_buf.shape[1]
    kw = jnp.concatenate([win_buf, kv6[:, :, 4:6]], axis=1)
    q_pos = past_len + jnp.arange(Tn)
    w_pos = past_len - wb + jnp.arange(wb + Tn)
    return nsa_attend(q, gate, q_pos, kv_cmp, kvs_t, kw, w_pos)


def sgu(u, v, ln_g, ln_b, w_s, b_s):
    N, T = u.shape[:2]
    L = min(T, SGU_CHUNK)
    vn = layernorm(v.reshape(N, T, U_W), ln_g, ln_b).reshape(N, T, SGU_GROUPS, SGU_DIM)
    w = jnp.tril(w_s[:, :L, :L])
    vc = vn.reshape(N, T // L, L, SGU_GROUPS, SGU_DIM)
    mix = jnp.einsum('gpq,ncqgd->ncpgd', w, vc) + b_s[:, :L].T[None, None, :, :, None]
    out = u.reshape(N, T // L, L, SGU_GROUPS, SGU_DIM) * mix
    return out.reshape(N, T, U_W), vn


def s5_discretize(a_re, a_im, log_step, b_re, b_im):
    a_re = a_re.astype(F32)
    a_im = a_im.astype(F32)
    dt = jnp.exp(log_step.astype(F32))[:, None]
    lr = a_re * dt
    li = a_im * dt
    mag = jnp.exp(lr)
    ab_re = mag * jnp.cos(li)
    ab_im = mag * jnp.sin(li)
    den = a_re * a_re + a_im * a_im
    nr = ab_re - 1.0
    cr = (nr * a_re + ab_im * a_im) / den
    cim = (ab_im * a_re - nr * a_im) / den
    b_re = b_re.astype(F32)
    b_im = b_im.astype(F32)
    bb_re = cr[..., None] * b_re - cim[..., None] * b_im
    bb_im = cr[..., None] * b_im + cim[..., None] * b_re
    return lr, li, ab_re, ab_im, bb_re, bb_im


def s5_chunk(h_re, h_im, u, disc, c_re, c_im):
    lr, li, ab_re, ab_im, bb_re, bb_im = disc
    N, L, _ = u.shape
    ug = u.reshape(N, L, S5_GROUPS, S5_GROUP)
    bu_re = jnp.einsum('nlgc,gpc->nlgp', ug, bb_re)
    bu_im = jnp.einsum('nlgc,gpc->nlgp', ug, bb_im)
    a_re = jnp.broadcast_to(ab_re, bu_re.shape)
    a_im = jnp.broadcast_to(ab_im, bu_re.shape)

    def combine(e1, e2):
        a1r, a1i, b1r, b1i = e1
        a2r, a2i, b2r, b2i = e2
        return (a2r * a1r - a2i * a1i, a2r * a1i + a2i * a1r,
                a2r * b1r - a2i * b1i + b2r, a2r * b1i + a2i * b1r + b2i)

    _, _, s_re, s_im = lax.associative_scan(combine, (a_re, a_im, bu_re, bu_im), axis=1)
    k = jnp.arange(1, L + 1, dtype=F32)[:, None, None]
    pm = jnp.exp(k * lr)
    p_re = pm * jnp.cos(k * li)
    p_im = pm * jnp.sin(k * li)
    hr = s_re + p_re * h_re[:, None] - p_im * h_im[:, None]
    hi = s_im + p_re * h_im[:, None] + p_im * h_re[:, None]
    y = jnp.einsum('nlgp,gcp->nlgc', hr, c_re.astype(F32)) - jnp.einsum('nlgp,gcp->nlgc', hi, c_im.astype(F32))
    return hr[:, -1], hi[:, -1], y.reshape(N, L, D_MODEL)


def s5_layer(h, h0_re, h0_im, disc, c_re, c_im, d, w_a, w_b):
    N, T, _ = h.shape
    L = min(T, S5_CHUNK)
    u = h.astype(F32)
    xs = jnp.moveaxis(u.reshape(N, T // L, L, D_MODEL), 1, 0)

    def step(carry, u_c):
        hr, hi, y_c = s5_chunk(carry[0], carry[1], u_c, disc, c_re, c_im)
        return (hr, hi), y_c

    (hr, hi), ys = lax.scan(step, (h0_re, h0_im), xs)
    y = jnp.moveaxis(ys, 0, 1).reshape(N, T, D_MODEL) + d.astype(F32) * u
    z = jax.nn.gelu(y).astype(h.dtype)
    out = (z @ w_a) * jax.nn.sigmoid(z @ w_b)
    return out, hr, hi


def swiglu(h, w1, w3, w2):
    return (jax.nn.silu(h @ w1) * (h @ w3)) @ w2


def setup_inputs(seed: int = 0) -> dict:
    key = jax.random.key(seed)
    ks = iter(jax.random.split(key, 40))

    def nrm(shape, scale=1.0):
        return jax.random.normal(next(ks), shape, F32) * scale

    n_pages = PAST_LEN // PAGE_SIZE
    n_used = DEC_BATCH * n_pages
    pool = (n_used * 5 + 3) // 4
    win_len = min(WINDOW, PAST_LEN)
    page_table = jax.random.permutation(next(ks), pool)[:n_used].reshape(DEC_BATCH, n_pages).astype(jnp.int32)
    a_im = jnp.broadcast_to(jnp.pi * jnp.arange(S5_STATE, dtype=F32), (N_ODD, S5_GROUPS, S5_STATE))
    inv2 = 0.5 ** 0.5
    return {
        'x_prompt': nrm((BATCH, SEQ, D_MODEL)),
        'x_sample': nrm((DEC_BATCH, DEC_SEQ, D_MODEL)),
        'cache_nsa_kv': nrm((N_EVEN, pool, PAGE_SIZE, 4, N_KV, HEAD_DIM)),
        'cache_nsa_win': nrm((N_EVEN, DEC_BATCH, win_len, 2, N_KV, HEAD_DIM)),
        'state_s5_re': nrm((N_ODD, DEC_BATCH, S5_GROUPS, S5_STATE), 0.5),
        'state_s5_im': nrm((N_ODD, DEC_BATCH, S5_GROUPS, S5_STATE), 0.5),
        'page_table': page_table,
        'norm_mix': 1.0 + nrm((DEPTH, D_MODEL), 0.01),
        'norm_ffn': 1.0 + nrm((DEPTH, D_MODEL), 0.01),
        'norm_final': 1.0 + nrm((D_MODEL,), 0.01),
        'w_in': nrm((N_EVEN, D_MODEL, IN_COLS), D_MODEL ** -0.5),
        'w_out': nrm((N_EVEN, MIX_W, D_MODEL), MIX_W ** -0.5),
        'cmp_pe': nrm((N_EVEN, 2, CMP_BLOCK, HEAD_DIM), 0.1),
        'cmp_w1': nrm((N_EVEN, 2, CMP_BLOCK, HEAD_DIM, HEAD_DIM), (CMP_BLOCK * HEAD_DIM) ** -0.5),
        'cmp_w2': nrm((N_EVEN, 2, HEAD_DIM, HEAD_DIM), HEAD_DIM ** -0.5),
        'sgu_ln_g': 1.0 + nrm((N_EVEN, U_W), 0.01),
        'sgu_ln_b': nrm((N_EVEN, U_W), 0.01),
        'sgu_w': nrm((N_EVEN, SGU_GROUPS, SGU_CHUNK, SGU_CHUNK), SGU_CHUNK ** -0.5),
        'sgu_b': 1.0 + nrm((N_EVEN, SGU_GROUPS, SGU_CHUNK), 0.1),
        's5_a_re': -0.5 + nrm((N_ODD, S5_GROUPS, S5_STATE), 0.01),
        's5_a_im': a_im + nrm((N_ODD, S5_GROUPS, S5_STATE), 0.01),
        's5_log_step': jax.random.uniform(next(ks), (N_ODD, S5_GROUPS), F32, math.log(1e-3), math.log(1e-1)),
        's5_b_re': nrm((N_ODD, S5_GROUPS, S5_STATE, S5_GROUP), inv2 * S5_GROUP ** -0.5),
        's5_b_im': nrm((N_ODD, S5_GROUPS, S5_STATE, S5_GROUP), inv2 * S5_GROUP ** -0.5),
        's5_c_re': nrm((N_ODD, S5_GROUPS, S5_GROUP, S5_STATE), inv2 * S5_STATE ** -0.5),
        's5_c_im': nrm((N_ODD, S5_GROUPS, S5_GROUP, S5_STATE), inv2 * S5_STATE ** -0.5),
        's5_d': nrm((N_ODD, D_MODEL), 0.5),
        'glu_w_a': nrm((N_ODD, D_MODEL, D_MODEL), D_MODEL ** -0.5),
        'glu_w_b': nrm((N_ODD, D_MODEL, D_MODEL), D_MODEL ** -0.5),
        'ffn_w1': nrm((DEPTH, D_MODEL, D_FF), D_MODEL ** -0.5),
        'ffn_w3': nrm((DEPTH, D_MODEL, D_FF), D_MODEL ** -0.5),
        'ffn_w2': nrm((DEPTH, D_FF, D_MODEL), D_FF ** -0.5),
    }


def reference(x_prompt, x_sample, cache_nsa_kv, cache_nsa_win, state_s5_re, state_s5_im, page_table,
              norm_mix, norm_ffn, norm_final, w_in, w_out, cmp_pe, cmp_w1, cmp_w2,
              sgu_ln_g, sgu_ln_b, sgu_w, sgu_b,
              s5_a_re, s5_a_im, s5_log_step, s5_b_re, s5_b_im, s5_c_re, s5_c_im, s5_d, glu_w_a, glu_w_b,
              ffn_w1, ffn_w3, ffn_w2):
    xp = x_prompt
    xs = x_sample
    bp, seq = xp.shape[:2]
    tn = xs.shape[1]
    past_len = page_table.shape[1] * cache_nsa_kv.shape[2]
    pos_p = jnp.arange(seq)
    pos_s = past_len + jnp.arange(tn)
    kv_p, kv_s, win_p, win_s, sgu_v_s = [], [], [], [], []
    s5p_re, s5p_im, s5s_re, s5s_im = [], [], [], []
    for layer in range(DEPTH):
        hp = rmsnorm(xp, norm_mix[layer])
        hs = rmsnorm(xs, norm_mix[layer])
        if layer % 2 == 0:
            e = layer // 2
            qp, gp, kvp6, up, vp = project_even(hp, pos_p, w_in[e])
            qs, gs, kvs6, us, vs = project_even(hs, pos_s, w_in[e])
            ap = nsa_prompt(qp, gp, kvp6, cmp_pe[e], cmp_w1[e], cmp_w2[e])
            a_s = nsa_sample(qs, gs, kvs6, cache_nsa_kv[e], cache_nsa_win[e], page_table,
                             cmp_pe[e], cmp_w1[e], cmp_w2[e])
            sp, _ = sgu(up, vp, sgu_ln_g[e], sgu_ln_b[e], sgu_w[e], sgu_b[e])
            ss, vn_s = sgu(us, vs, sgu_ln_g[e], sgu_ln_b[e], sgu_w[e], sgu_b[e])
            xp = xp + jnp.concatenate([ap, sp], axis=-1) @ w_out[e]
            xs = xs + jnp.concatenate([a_s, ss], axis=-1) @ w_out[e]
            kv_p.append(kvp6[:, :, 0:4])
            kv_s.append(kvs6[:, :, 0:4])
            win_p.append(kvp6[:, seq - min(WINDOW, seq):, 4:6])
            win_s.append(kvs6[:, :, 4:6])
            sgu_v_s.append(vn_s)
        else:
            o = layer // 2
            disc = s5_discretize(s5_a_re[o], s5_a_im[o], s5_log_step[o], s5_b_re[o], s5_b_im[o])
            h0 = jnp.zeros((bp, S5_GROUPS, S5_STATE), F32)
            mp, hpr, hpi = s5_layer(hp, h0, h0, disc, s5_c_re[o], s5_c_im[o], s5_d[o], glu_w_a[o], glu_w_b[o])
            ms, hsr, hsi = s5_layer(hs, state_s5_re[o].astype(F32), state_s5_im[o].astype(F32), disc,
                                    s5_c_re[o], s5_c_im[o], s5_d[o], glu_w_a[o], glu_w_b[o])
            xp = xp + mp
            xs = xs + ms
            s5p_re.append(hpr)
            s5p_im.append(hpi)
            s5s_re.append(hsr)
            s5s_im.append(hsi)
        xp = xp + swiglu(rmsnorm(xp, norm_ffn[layer]), ffn_w1[layer], ffn_w3[layer], ffn_w2[layer])
        xs = xs + swiglu(rmsnorm(xs, norm_ffn[layer]), ffn_w1[layer], ffn_w3[layer], ffn_w2[layer])
    y_prompt = rmsnorm(xp, norm_final)
    y_sample = rmsnorm(xs, norm_final)
    return (y_prompt, y_sample, jnp.stack(kv_p), jnp.stack(kv_s), jnp.stack(win_p), jnp.stack(win_s),
            jnp.stack(sgu_v_s), jnp.stack(s5p_re), jnp.stack(s5p_im), jnp.stack(s5s_re), jnp.stack(s5s_im))
```

```python
import functools
import math

import jax
import jax.numpy as jnp
from jax import lax
from jax.experimental import pallas as pl
from jax.experimental.pallas import tpu as pltpu

F32 = jnp.float32
BF16 = jnp.bfloat16

D_MODEL = 1024
DEPTH = 4
N_HEADS = 8
N_KV = 2
GQ = N_HEADS // N_KV
HEAD_DIM = 64
ROPE_DIM = 16
ROPE_THETA = 500000.0
CMP_BLOCK = 32
CMP_STRIDE = 16
SLC_BLOCK = 64
N_SELECT = 16
WINDOW = 512
Q_BLOCK = 128
SGU_GROUPS = 4
SGU_DIM = 128
SGU_CHUNK = 128
Q_W = N_HEADS * HEAD_DIM
KV_W = 6 * N_KV * HEAD_DIM
GATE_W = 3 * N_HEADS
U_W = SGU_GROUPS * SGU_DIM
S5_GROUP = 16
S5_GROUPS = D_MODEL // S5_GROUP
S5_STATE = 64
S5_W = S5_GROUPS * S5_STATE
D_FF = 2816
EPS = 1e-6
NEG_INF = -1e30
TINY = 1e-30
FORCE_SCORE = 1e4

LANES = 128
GATE_PAD = LANES
W_IN_COLS = Q_W + KV_W + 2 * U_W + GATE_PAD
FF_CHUNK = 256
TOKEN_TILE = 512
KEY_TILE = 512
S5_CHUNK_ROWS = 256
S5_SEQ_GROUP = 32
S5_LANE_CHUNK = 1024
NSA_SEQ_GROUP = 16
VMEM_LIMIT = 56 * 1024 * 1024


def _dot(a, b):
    return jnp.dot(a, b, preferred_element_type=F32)


def _dot_nt(a, b):
    return lax.dot_general(a, b, (((1,), (1,)), ((), ())), preferred_element_type=F32)


def _rms(x, g):
    return x * lax.rsqrt(jnp.mean(x * x, axis=-1, keepdims=True) + EPS) * g


def _sigmoid(x):
    return 1.0 / (1.0 + jnp.exp(-x))


def _params(sem):
    return pltpu.CompilerParams(dimension_semantics=sem, vmem_limit_bytes=VMEM_LIMIT)


def _full(shape):
    n = len(shape)
    return pl.BlockSpec(shape, lambda *_: (0,) * n)


def _inproj_kernel(x_ref, g_ref, w_ref, cos_ref, sin_ref, lng_ref, lnb_ref,
                   q_ref, kv_ref, kvb_ref, gate_ref, u_ref, vn_ref):
    x = x_ref[...]
    h = _rms(x, g_ref[...]).astype(BF16)
    cos = cos_ref[...]
    sin = sin_ref[...]
    lane = lax.broadcasted_iota(jnp.int32, cos.shape, 1)
    low = (lane % HEAD_DIM) < (ROPE_DIM // 2)

    def rope(t):
        rot = jnp.where(low, pltpu.roll(t, LANES - ROPE_DIM // 2, 1), pltpu.roll(t, ROPE_DIM // 2, 1))
        return t * cos + rot * sin

    pq = _dot(h, w_ref[:, 0:Q_W])
    for j in range(Q_W // LANES):
        q_ref[:, j * LANES:(j + 1) * LANES] = rope(pq[:, j * LANES:(j + 1) * LANES]) * (HEAD_DIM ** -0.5)
    pkv = _dot(h, w_ref[:, Q_W:Q_W + KV_W])
    for j in range(KV_W // LANES):
        t = pkv[:, j * LANES:(j + 1) * LANES]
        if j % 2 == 0:
            t = rope(t)
        kv_ref[:, j * LANES:(j + 1) * LANES] = t
        if j >= 2:
            kvb_ref[:, (j - 2) * LANES:(j - 1) * LANES] = t.astype(BF16)
    o = Q_W + KV_W
    u_ref[...] = jax.nn.gelu(_dot(h, w_ref[:, o:o + U_W]))
    v = jax.nn.gelu(_dot(h, w_ref[:, o + U_W:o + 2 * U_W]))
    vc = v - jnp.mean(v, axis=-1, keepdims=True)
    var = jnp.mean(vc * vc, axis=-1, keepdims=True)
    vn_ref[...] = vc * lax.rsqrt(var + EPS) * lng_ref[...] + lnb_ref[...]
    gate_ref[...] = _sigmoid(_dot(h, w_ref[:, o + 2 * U_W:o + 2 * U_W + GATE_PAD]))


def _inproj(x, g, w, cos, sin, lng, lnb, tm):
    T = x.shape[0]
    nt = T // tm
    ntab = cos.shape[0] // tm
    row = lambda w_: pl.BlockSpec((tm, w_), lambda i: (i, 0))
    tab = pl.BlockSpec((tm, LANES), lambda i: (i % ntab, 0))
    return pl.pallas_call(
        _inproj_kernel,
        grid=(nt,),
        in_specs=[row(D_MODEL), _full((1, D_MODEL)), _full((D_MODEL, W_IN_COLS)), tab, tab,
                  _full((1, U_W)), _full((1, U_W))],
        out_specs=[row(Q_W), row(KV_W), row(4 * LANES), row(GATE_PAD), row(U_W), row(U_W)],
        out_shape=[jax.ShapeDtypeStruct((T, Q_W), F32), jax.ShapeDtypeStruct((T, KV_W), F32),
                   jax.ShapeDtypeStruct((T, 4 * LANES), BF16), jax.ShapeDtypeStruct((T, GATE_PAD), F32),
                   jax.ShapeDtypeStruct((T, U_W), F32), jax.ShapeDtypeStruct((T, U_W), F32)],
        compiler_params=_params(("parallel",)),
        name="inproj",
    )(x, g, w, cos, sin, lng, lnb)


def _compress_core(load_rows, nch, pe_ref, w1_ref, w2_ref, acc0_ref, acc1_ref, k_ref, v_ref):
    for s in range(CMP_STRIDE):
        xs = load_rows(s)
        for m, acc in ((0, acc0_ref), (1, acc1_ref)):
            idx = m * CMP_STRIDE + s
            part = _dot((xs + pe_ref[idx:idx + 1, :]).astype(BF16), w1_ref[idx])
            if s == 0:
                acc[...] = part
            else:
                acc[...] += part
    hid = acc0_ref[...] + pltpu.roll(acc1_ref[...], nch - 1, 0)
    out = _dot(jax.nn.gelu(hid).astype(BF16), w2_ref[...])
    k_ref[...] = out[:, 0:LANES].astype(BF16).reshape(k_ref.shape)
    v_ref[...] = out[:, LANES:2 * LANES].astype(BF16).reshape(v_ref.shape)


def _compress_prompt_kernel(xk_ref, xv_ref, pe_ref, w1_ref, w2_ref, k_ref, v_ref, acc0_ref, acc1_ref):
    nch = xk_ref.shape[0] // CMP_STRIDE
    rows = lambda s: pl.ds(s, nch, stride=CMP_STRIDE)
    load = lambda s: jnp.concatenate([xk_ref[rows(s), :], xv_ref[rows(s), :]], axis=1)
    _compress_core(load, nch, pe_ref, w1_ref, w2_ref, acc0_ref, acc1_ref, k_ref, v_ref)


def _compress_prompt(kv, B, S, pe, w1, w2):
    nch = S // CMP_STRIDE
    out = jax.ShapeDtypeStruct((B, nch, LANES), BF16)
    return pl.pallas_call(
        _compress_prompt_kernel,
        grid=(B,),
        in_specs=[pl.BlockSpec((S, LANES), lambda b: (b, 0)), pl.BlockSpec((S, LANES), lambda b: (b, 1)),
                  _full(pe.shape), _full(w1.shape), _full(w2.shape)],
        out_specs=[pl.BlockSpec((1, nch, LANES), lambda b: (b, 0, 0))] * 2,
        out_shape=[out, out],
        scratch_shapes=[pltpu.VMEM((nch, 2 * LANES), F32)] * 2,
        compiler_params=_params(("parallel",)),
        name="compress_prompt",
    )(kv, kv, pe, w1, w2)


def _compress_sample_kernel(n_pages, pt_ref, *refs):
    pages = refs[:n_pages]
    pe_ref, w1_ref, w2_ref, k_ref, v_ref, acc0_ref, acc1_ref = refs[n_pages:]
    nch = n_pages * pages[0].shape[0]
    row_w = 4 * LANES

    def load(s):
        return jnp.concatenate([p[:, s * row_w:s * row_w + 2 * LANES] for p in pages], axis=0)

    _compress_core(load, nch, pe_ref, w1_ref, w2_ref, acc0_ref, acc1_ref, k_ref, v_ref)


def _page_specs(n_pages, page, base, lane_block):
    def spec(j):
        return pl.BlockSpec((1, page, 2 * LANES),
                            lambda n, pt: (base + pt[n * n_pages + j], 0, lane_block))
    return [spec(j) for j in range(n_pages)]


def _compress_sample(cache, pt_flat, NB, n_pages, page, base, pe, w1, w2):
    per = page // CMP_STRIDE
    nch = n_pages * per
    out = jax.ShapeDtypeStruct((NB * nch, LANES), BF16)
    cfull = lambda shape: pl.BlockSpec(shape, lambda n, pt: (0,) * len(shape))
    chunked = cache.reshape(cache.shape[0] * per, CMP_STRIDE * cache.shape[2])

    def page_spec(j):
        return pl.BlockSpec((per, chunked.shape[1]), lambda n, pt: (base + pt[n * n_pages + j], 0))

    grid_spec = pltpu.PrefetchScalarGridSpec(
        num_scalar_prefetch=1,
        grid=(NB,),
        in_specs=[page_spec(j) for j in range(n_pages)] + [cfull(pe.shape), cfull(w1.shape), cfull(w2.shape)],
        out_specs=[pl.BlockSpec((nch, LANES), lambda n, pt: (n, 0))] * 2,
        scratch_shapes=[pltpu.VMEM((nch, 2 * LANES), F32)] * 2,
    )
    return pl.pallas_call(
        functools.partial(_compress_sample_kernel, n_pages),
        grid_spec=grid_spec,
        out_shape=[out, out],
        compiler_params=_params(("parallel",)),
        name="compress_sample",
    )(pt_flat, *([chunked] * n_pages), pe, w1, w2)


def _group_queries(q, kv):
    lane = lax.broadcasted_iota(jnp.int32, (q.shape[0], LANES), 1)
    mine = (lane < HEAD_DIM) if kv == 0 else (lane >= HEAD_DIM)
    parts = []
    for g in range(GQ):
        h = kv * GQ + g
        t = q[:, (h // 2) * LANES:(h // 2 + 1) * LANES]
        if h % 2 != kv:
            t = pltpu.roll(t, HEAD_DIM, 1)
        parts.append(jnp.where(mine, t, 0.0))
    return jnp.concatenate(parts, axis=0).astype(BF16)


def _softmax_rows(s, mask):
    s = jnp.where(mask, s, NEG_INF)
    m = jnp.max(s, axis=-1, keepdims=True)
    e = jnp.where(mask, jnp.exp(s - m), 0.0)
    return e / jnp.maximum(jnp.sum(e, axis=-1, keepdims=True), TINY)


def _select_blocks(score):
    lane = lax.broadcasted_iota(jnp.int32, score.shape, 1).astype(F32)

    def body(_, carry):
        sc, sel = carry
        m = jnp.max(sc, axis=-1, keepdims=True)
        first = jnp.min(jnp.where(sc == m, lane, float(LANES)), axis=-1, keepdims=True)
        hit = lane == first
        return jnp.where(hit, -2.0, sc), jnp.where(hit, 1.0, sel)

    _, sel = lax.fori_loop(0, N_SELECT, body, (score, jnp.zeros_like(score)))
    return sel


def _block_scores(imp, tq, n_slc):
    blk = lax.broadcasted_iota(jnp.int32, imp.shape, 1)
    cur = tq // SLC_BLOCK
    forced = (blk == 0) | (blk == cur) | (blk == cur - 1)
    allowed = blk * SLC_BLOCK <= tq
    score = jnp.where(forced, FORCE_SCORE, jnp.where(allowed, imp, -1.0))
    return jnp.where(blk < n_slc, score, -3.0)


def _cover(ci, sj):
    return ((ci * CMP_STRIDE < (sj + 1) * SLC_BLOCK) & (ci * CMP_STRIDE + CMP_BLOCK > sj * SLC_BLOCK))


def _nsa_prompt_kernel(q_ref, gate_ref, ck_ref, cv_ref, kvb_ref, e_ref, out_ref, *, seq):
    i = pl.program_id(1)
    R = Q_BLOCK
    s0 = i * R
    q = q_ref[...]
    gate = gate_ref[...]
    ncb = ck_ref.shape[1]
    n_cmp = seq // CMP_STRIDE - 1
    n_slc = seq // SLC_BLOCK
    tok4 = lax.broadcasted_iota(jnp.int32, (GQ * R, 1), 0) % R
    tq4 = s0 + tok4
    tq = s0 + lax.broadcasted_iota(jnp.int32, (R, 1), 0)
    ck = ck_ref[0]
    cv = cv_ref[0]
    ci = lax.broadcasted_iota(jnp.int32, (ncb, LANES), 0)
    sj = lax.broadcasted_iota(jnp.int32, (ncb, LANES), 1)
    cover = jnp.where(_cover(ci, sj) & (ci < n_cmp) & (sj < n_slc), 1.0, 0.0).astype(BF16)
    cmp_i = lax.broadcasted_iota(jnp.int32, (GQ * R, ncb), 1)
    cmp_mask = (cmp_i * CMP_STRIDE + (CMP_BLOCK - 1) <= tq4) & (cmp_i < n_cmp)
    n_tiles = (s0 + R + KEY_TILE - 1) // KEY_TILE
    wstart = pl.multiple_of(jnp.maximum(s0 - WINDOW, 0), R)
    wlen = WINDOW + R
    wpos = wstart + lax.broadcasted_iota(jnp.int32, (GQ * R, wlen), 1)
    wdiff = tq4 - wpos
    wmask = (wdiff >= 0) & (wdiff < WINDOW)
    kcol = lax.broadcasted_iota(jnp.int32, (GQ * R, KEY_TILE), 1)

    for kv in range(N_KV):
        qz = _group_queries(q, kv)
        p_c = _softmax_rows(_dot_nt(qz, ck), cmp_mask).astype(BF16)
        o_c = _dot(p_c, cv)
        imp4 = _dot(p_c, cover)
        imp = imp4[0:R] + imp4[R:2 * R] + imp4[2 * R:3 * R] + imp4[3 * R:4 * R]
        sel = _select_blocks(_block_scores(imp, tq, n_slc))
        selneg = jnp.where(sel > 0.5, 0.0, NEG_INF).astype(BF16)

        def tile_step(t, carry):
            m, l, acc = carry
            off = pl.multiple_of(t * KEY_TILE, KEY_TILE)
            bias = _dot(selneg, e_ref[:, pl.ds(off, KEY_TILE)])
            bias4 = jnp.concatenate([bias] * GQ, axis=0)
            kt = kvb_ref[pl.ds(off, KEY_TILE), 0:LANES]
            vt = kvb_ref[pl.ds(off, KEY_TILE), LANES:2 * LANES]
            s = _dot_nt(qz, kt) + bias4
            s = jnp.where(off + kcol <= tq4, s, NEG_INF)
            m_new = jnp.maximum(m, jnp.max(s, axis=-1, keepdims=True))
            alpha = jnp.exp(m - m_new)
            p = jnp.exp(s - m_new)
            l = alpha * l + jnp.sum(p, axis=-1, keepdims=True)
            acc = alpha * acc + _dot(p.astype(BF16), vt)
            return m_new, l, acc

        init = (jnp.full((GQ * R, 1), NEG_INF, F32), jnp.zeros((GQ * R, 1), F32),
                jnp.zeros((GQ * R, LANES), F32))
        _, l, acc = lax.fori_loop(0, n_tiles, tile_step, init)
        o_s = acc / jnp.maximum(l, TINY)

        kw = kvb_ref[pl.ds(wstart, wlen), 2 * LANES:3 * LANES]
        vw = kvb_ref[pl.ds(wstart, wlen), 3 * LANES:4 * LANES]
        p_w = _softmax_rows(_dot_nt(qz, kw), wmask).astype(BF16)
        o_w = _dot(p_w, vw)

        for g in range(GQ):
            h = kv * GQ + g
            rows = slice(g * R, (g + 1) * R)
            lanes = slice(kv * HEAD_DIM, (kv + 1) * HEAD_DIM)
            o = (gate[:, 3 * h:3 * h + 1] * o_c[rows, lanes]
                 + gate[:, 3 * h + 1:3 * h + 2] * o_s[rows, lanes]
                 + gate[:, 3 * h + 2:3 * h + 3] * o_w[rows, lanes])
            out_ref[:, h * HEAD_DIM:(h + 1) * HEAD_DIM] = o


def _nsa_prompt(q, gate, ck, cv, kvb, emat, B, S):
    nq = S // Q_BLOCK
    ncb = ck.shape[1]
    rowblk = lambda w_: pl.BlockSpec((Q_BLOCK, w_), lambda b, i: (b * nq + i, 0))
    return pl.pallas_call(
        functools.partial(_nsa_prompt_kernel, seq=S),
        grid=(B, nq),
        in_specs=[rowblk(Q_W), rowblk(GATE_PAD),
                  pl.BlockSpec((1, ncb, LANES), lambda b, i: (b, 0, 0)),
                  pl.BlockSpec((1, ncb, LANES), lambda b, i: (b, 0, 0)),
                  pl.BlockSpec((S, 4 * LANES), lambda b, i: (b, 0)),
                  pl.BlockSpec(emat.shape, lambda b, i: (0, 0))],
        out_specs=rowblk(Q_W),
        out_shape=jax.ShapeDtypeStruct((B * S, Q_W), F32),
        compiler_params=_params(("parallel", "arbitrary")),
        name="nsa_prompt",
    )(q, gate, ck, cv, kvb, emat)


def _nsa_sample_select_kernel(q_ref, gate_ref, ck_ref, cv_ref, sel_ref, oc_ref, *, past_len, tn, ncb):
    R = q_ref.shape[0]
    q = q_ref[...]
    gate = gate_ref[...]
    total = past_len + tn
    n_cmp = total // CMP_STRIDE - 1
    n_slc = -(-total // SLC_BLOCK)
    ncol = ck_ref.shape[0]
    row4 = lax.broadcasted_iota(jnp.int32, (GQ * R, 1), 0) % R
    tq4 = past_len + row4 % tn
    tq = past_len + lax.broadcasted_iota(jnp.int32, (R, 1), 0) % tn
    col = lax.broadcasted_iota(jnp.int32, (GQ * R, ncol), 1)
    ci = col % ncb
    cmp_mask = ((col // ncb == row4 // tn) & (ci * CMP_STRIDE + (CMP_BLOCK - 1) <= tq4) & (ci < n_cmp))
    cr = lax.broadcasted_iota(jnp.int32, (ncol, LANES), 0) % ncb
    sj = lax.broadcasted_iota(jnp.int32, (ncol, LANES), 1)
    cover = jnp.where(_cover(cr, sj) & (cr < n_cmp) & (sj < n_slc), 1.0, 0.0).astype(BF16)
    ck = ck_ref[...]
    cv = cv_ref[...]
    for kv in range(N_KV):
        qz = _group_queries(q, kv)
        p_c = _softmax_rows(_dot_nt(qz, ck), cmp_mask).astype(BF16)
        o_c = _dot(p_c, cv)
        imp4 = _dot(p_c, cover)
        imp = imp4[0:R] + imp4[R:2 * R] + imp4[2 * R:3 * R] + imp4[3 * R:4 * R]
        sel = _select_blocks(_block_scores(imp, tq, n_slc))
        sel_ref[:, kv * LANES:(kv + 1) * LANES] = jnp.where(sel > 0.5, 0.0, NEG_INF)
        for g in range(GQ):
            h = kv * GQ + g
            oc_ref[:, h * HEAD_DIM:(h + 1) * HEAD_DIM] = (
                gate[:, 3 * h:3 * h + 1] * o_c[g * R:(g + 1) * R, kv * HEAD_DIM:(kv + 1) * HEAD_DIM])


def _nsa_sample_select(q, gate, ck, cv, NB, tn, past_len, ncb):
    R = NSA_SEQ_GROUP * tn
    rowblk = lambda w_: pl.BlockSpec((R, w_), lambda i: (i, 0))
    cblk = pl.BlockSpec((NSA_SEQ_GROUP * ncb, LANES), lambda i: (i, 0))
    return pl.pallas_call(
        functools.partial(_nsa_sample_select_kernel, past_len=past_len, tn=tn, ncb=ncb),
        grid=(NB // NSA_SEQ_GROUP,),
        in_specs=[rowblk(Q_W), rowblk(GATE_PAD), cblk, cblk],
        out_specs=[rowblk(2 * LANES), rowblk(Q_W)],
        out_shape=[jax.ShapeDtypeStruct((NB * tn, 2 * LANES), F32),
                   jax.ShapeDtypeStruct((NB * tn, Q_W), F32)],
        compiler_params=_params(("parallel",)),
        name="nsa_sample_select",
    )(q, gate, ck, cv)


def _nsa_sample_attend_kernel(n_pages, pt_ref, *refs, past_len, tn):
    pages = refs[:n_pages]
    q_ref, gate_ref, sel_ref, oc_ref, kvn_ref, win_ref, e_ref, out_ref = refs[n_pages:]
    R = tn
    q = q_ref[...]
    gate = gate_ref[...]
    kvn = kvn_ref[...]
    wb = win_ref.shape[1]
    new_blk = past_len // SLC_BLOCK
    tok4 = lax.broadcasted_iota(jnp.int32, (GQ * R, 1), 0) % R
    newcol = lax.broadcasted_iota(jnp.int32, (GQ * R, LANES), 1)
    causal_new = newcol <= tok4
    k_past = jnp.concatenate([p[0, :, 0:LANES] for p in pages], axis=0).astype(BF16)
    v_past = jnp.concatenate([p[0, :, LANES:2 * LANES] for p in pages], axis=0).astype(BF16)
    pad = jnp.zeros((LANES - tn, KV_W), F32)
    kvn = jnp.concatenate([kvn, pad], axis=0)
    k_new = kvn[:, 2 * LANES:3 * LANES].astype(BF16)
    v_new = kvn[:, 3 * LANES:4 * LANES].astype(BF16)
    kw = win_ref[0, :, 0:LANES].astype(BF16)
    vw = win_ref[0, :, LANES:2 * LANES].astype(BF16)
    kw_new = kvn[:, 4 * LANES:5 * LANES].astype(BF16)
    vw_new = kvn[:, 5 * LANES:6 * LANES].astype(BF16)
    wcol = lax.broadcasted_iota(jnp.int32, (GQ * R, wb), 1)
    wdiff = wb + tok4 - wcol
    wmask = (wdiff >= 0) & (wdiff < WINDOW) & (wcol >= wb - past_len)

    for kv in range(N_KV):
        qz = _group_queries(q, kv)
        sel4 = jnp.concatenate([sel_ref[:, kv * LANES:(kv + 1) * LANES]] * GQ, axis=0)
        s = _dot_nt(qz, k_past) + _dot(sel4.astype(BF16), e_ref[...])
        s_new = jnp.where(causal_new, _dot_nt(qz, k_new) + sel4[:, new_blk:new_blk + 1], NEG_INF)
        m = jnp.maximum(jnp.max(s, axis=-1, keepdims=True), jnp.max(s_new, axis=-1, keepdims=True))
        e = jnp.exp(s - m)
        e_new = jnp.exp(s_new - m)
        l = jnp.sum(e, axis=-1, keepdims=True) + jnp.sum(e_new, axis=-1, keepdims=True)
        o_s = (_dot(e.astype(BF16), v_past) + _dot(e_new.astype(BF16), v_new)) / jnp.maximum(l, TINY)

        s_w = jnp.where(wmask, _dot_nt(qz, kw), NEG_INF)
        s_wn = jnp.where(causal_new, _dot_nt(qz, kw_new), NEG_INF)
        m = jnp.maximum(jnp.max(s_w, axis=-1, keepdims=True), jnp.max(s_wn, axis=-1, keepdims=True))
        e = jnp.exp(s_w - m)
        e_new = jnp.exp(s_wn - m)
        l = jnp.sum(e, axis=-1, keepdims=True) + jnp.sum(e_new, axis=-1, keepdims=True)
        o_w = (_dot(e.astype(BF16), vw) + _dot(e_new.astype(BF16), vw_new)) / jnp.maximum(l, TINY)

        for g in range(GQ):
            h = kv * GQ + g
            rows = slice(g * R, (g + 1) * R)
            lanes = slice(kv * HEAD_DIM, (kv + 1) * HEAD_DIM)
            hl = slice(h * HEAD_DIM, (h + 1) * HEAD_DIM)
            out_ref[:, hl] = (oc_ref[:, hl] + gate[:, 3 * h + 1:3 * h + 2] * o_s[rows, lanes]
                              + gate[:, 3 * h + 2:3 * h + 3] * o_w[rows, lanes])


def _nsa_sample_attend(cache, win, pt_flat, q, gate, selneg, ocg, kvn, emat,
                       NB, tn, n_pages, page, base, win_base):
    past_len = n_pages * page
    wb = win.shape[1]
    rowblk = lambda w_: pl.BlockSpec((tn, w_), lambda n, pt: (n, 0))
    grid_spec = pltpu.PrefetchScalarGridSpec(
        num_scalar_prefetch=1,
        grid=(NB,),
        in_specs=_page_specs(n_pages, page, base, 1) + [
            rowblk(Q_W), rowblk(GATE_PAD), rowblk(2 * LANES), rowblk(Q_W), rowblk(KV_W),
            pl.BlockSpec((1, wb, 2 * LANES), lambda n, pt: (win_base + n, 0, 0)),
            pl.BlockSpec(emat.shape, lambda n, pt: (0, 0))],
        out_specs=rowblk(Q_W),
    )
    return pl.pallas_call(
        functools.partial(_nsa_sample_attend_kernel, n_pages, past_len=past_len, tn=tn),
        grid_spec=grid_spec,
        out_shape=jax.ShapeDtypeStruct((NB * tn, Q_W), F32),
        compiler_params=_params(("parallel",)),
        name="nsa_sample_attend",
    )(pt_flat, *([cache] * n_pages), q, gate, selneg, ocg, kvn, win, emat)


def _ffn(x1, gf_ref, w1_ref, w3_ref, w2_ref, hb_ref, acc_ref):
    hb_ref[...] = _rms(x1, gf_ref[...]).astype(BF16)
    acc_ref[...] = x1

    def body(c, carry):
        off = pl.multiple_of(c * FF_CHUNK, FF_CHUNK)
        h = hb_ref[...]
        a = _dot(h, w1_ref[:, pl.ds(off, FF_CHUNK)])
        b = _dot(h, w3_ref[:, pl.ds(off, FF_CHUNK)])
        gact = (a * _sigmoid(a) * b).astype(BF16)
        acc_ref[...] += _dot(gact, w2_ref[pl.ds(off, FF_CHUNK), :])
        return carry

    lax.fori_loop(0, D_FF // FF_CHUNK, body, 0)
    return acc_ref[...]


def _finish(x2, final, gfin_ref, out_ref):
    out_ref[...] = _rms(x2, gfin_ref[...]) if final else x2


def _even_tail_kernel(x_ref, attn_ref, u_ref, vn_ref, wmix_ref, bmix_ref, wo_ref,
                      gf_ref, w1_ref, w3_ref, w2_ref, gfin_ref, out_ref,
                      sgu_ref, hb_ref, acc_ref, *, chunk, final):
    tm = x_ref.shape[0]
    r = lax.broadcasted_iota(jnp.int32, (SGU_CHUNK, SGU_CHUNK), 0)
    c = lax.broadcasted_iota(jnp.int32, (SGU_CHUNK, SGU_CHUNK), 1)
    causal = (c <= r) & (r // chunk == c // chunk)
    for g in range(SGU_GROUPS):
        w = jnp.where(causal, wmix_ref[g], 0.0).astype(BF16)
        b = bmix_ref[g]
        lanes = slice(g * SGU_DIM, (g + 1) * SGU_DIM)
        for k in range(tm // SGU_CHUNK):
            rows = slice(k * SGU_CHUNK, (k + 1) * SGU_CHUNK)
            mix = _dot(w, vn_ref[rows, lanes].astype(BF16)) + b
            sgu_ref[rows, lanes] = (u_ref[rows, lanes] * mix).astype(BF16)
    x1 = (x_ref[...] + _dot(attn_ref[...].astype(BF16), wo_ref[0:Q_W, :])
          + _dot(sgu_ref[...], wo_ref[Q_W:Q_W + U_W, :]))
    _finish(_ffn(x1, gf_ref, w1_ref, w3_ref, w2_ref, hb_ref, acc_ref), final, gfin_ref, out_ref)


def _odd_tail_kernel(x_ref, z_ref, wa_ref, wb_ref, gf_ref, w1_ref, w3_ref, w2_ref, gfin_ref, out_ref,
                     hb_ref, acc_ref, *, final):
    z = z_ref[...]
    x1 = x_ref[...] + _dot(z, wa_ref[...]) * _sigmoid(_dot(z, wb_ref[...]))
    _finish(_ffn(x1, gf_ref, w1_ref, w3_ref, w2_ref, hb_ref, acc_ref), final, gfin_ref, out_ref)


def _ffn_specs():
    return [_full((1, D_MODEL)), _full((D_MODEL, D_FF)), _full((D_MODEL, D_FF)), _full((D_FF, D_MODEL)),
            _full((1, D_MODEL))]


def _even_tail(x, attn, u, vn, wmix, bmix, wo, gf, w1, w3, w2, gfin, chunk, final):
    T = x.shape[0]
    tm = min(TOKEN_TILE, T)
    row = lambda w_: pl.BlockSpec((tm, w_), lambda i: (i, 0))
    return pl.pallas_call(
        functools.partial(_even_tail_kernel, chunk=chunk, final=final),
        grid=(T // tm,),
        in_specs=[row(D_MODEL), row(Q_W), row(U_W), row(U_W), _full(wmix.shape), _full(bmix.shape),
                  _full(wo.shape)] + _ffn_specs(),
        out_specs=row(D_MODEL),
        out_shape=jax.ShapeDtypeStruct((T, D_MODEL), F32),
        scratch_shapes=[pltpu.VMEM((tm, U_W), BF16), pltpu.VMEM((tm, D_MODEL), BF16),
                        pltpu.VMEM((tm, D_MODEL), F32)],
        compiler_params=_params(("parallel",)),
        name="even_tail",
    )(x, attn, u, vn, wmix, bmix, wo, gf, w1, w3, w2, gfin)


def _odd_tail(x, z, wa, wb, gf, w1, w3, w2, gfin, final):
    T = x.shape[0]
    tm = min(TOKEN_TILE, T)
    row = lambda w_: pl.BlockSpec((tm, w_), lambda i: (i, 0))
    return pl.pallas_call(
        functools.partial(_odd_tail_kernel, final=final),
        grid=(T // tm,),
        in_specs=[row(D_MODEL), row(D_MODEL), _full(wa.shape), _full(wb.shape)] + _ffn_specs(),
        out_specs=row(D_MODEL),
        out_shape=jax.ShapeDtypeStruct((T, D_MODEL), F32),
        scratch_shapes=[pltpu.VMEM((tm, D_MODEL), BF16), pltpu.VMEM((tm, D_MODEL), F32)],
        compiler_params=_params(("parallel",)),
        name="odd_tail",
    )(x, z, wa, wb, gf, w1, w3, w2, gfin)


def _s5_disc_kernel(are_ref, aim_ref, ls_ref, bre_ref, bim_ref, abre_ref, abim_ref, bbre_ref, bbim_ref):
    a_re = are_ref[...]
    a_im = aim_ref[...]
    dt = jnp.exp(ls_ref[...])
    lr = a_re * dt
    li = a_im * dt
    mag = jnp.exp(lr)
    ab_re = mag * jnp.cos(li)
    ab_im = mag * jnp.sin(li)
    den = a_re * a_re + a_im * a_im
    nr = ab_re - 1.0
    cr = (nr * a_re + ab_im * a_im) / den
    cim = (ab_im * a_re - nr * a_im) / den
    b_re = bre_ref[...]
    b_im = bim_ref[...]
    abre_ref[...] = ab_re
    abim_ref[...] = ab_im
    bbre_ref[...] = cr * b_re - cim * b_im
    bbim_ref[...] = cr * b_im + cim * b_re


def _s5_disc(a_re, a_im, log_step, b_re, b_im):
    col = jax.ShapeDtypeStruct((S5_W, 1), F32)
    mat = jax.ShapeDtypeStruct((S5_W, S5_GROUP), F32)
    ls = jnp.broadcast_to(log_step[:, None], (S5_GROUPS, S5_STATE)).reshape(S5_W, 1)
    return pl.pallas_call(
        _s5_disc_kernel,
        out_shape=[col, col, mat, mat],
        name="s5_disc",
    )(a_re.reshape(S5_W, 1), a_im.reshape(S5_W, 1), ls,
      b_re.reshape(S5_W, S5_GROUP), b_im.reshape(S5_W, S5_GROUP))


S5_BLOCKS = 4
S5_BLK_CH = D_MODEL // S5_BLOCKS
S5_BLK_ST = S5_W // S5_BLOCKS


def _s5_input_states(ub, bw_ref, bu_ref):
    for k in range(S5_BLOCKS):
        r = _dot(ub[:, k * S5_BLK_CH:(k + 1) * S5_BLK_CH], bw_ref[k])
        bu_ref[:, k * S5_BLK_ST:(k + 1) * S5_BLK_ST] = r[:, 0:S5_BLK_ST]
        bu_ref[:, S5_W + k * S5_BLK_ST:S5_W + (k + 1) * S5_BLK_ST] = r[:, S5_BLK_ST:2 * S5_BLK_ST]


def _s5_output(u, h_ref, cre_ref, cim_ref, d_ref, store):
    for k in range(S5_BLOCKS):
        hr = h_ref[:, k * S5_BLK_ST:(k + 1) * S5_BLK_ST].astype(BF16)
        hi = h_ref[:, S5_W + k * S5_BLK_ST:S5_W + (k + 1) * S5_BLK_ST].astype(BF16)
        cols = slice(k * S5_BLK_CH, (k + 1) * S5_BLK_CH)
        y = _dot(hr, cre_ref[k]) - _dot(hi, cim_ref[k]) + d_ref[:, cols] * u[:, cols]
        store(cols, jax.nn.gelu(y).astype(BF16))


def _s5_prompt_kernel(x_ref, g_ref, ab_ref, bw_ref, cre_ref, cim_ref, d_ref,
                      z_ref, hre_ref, him_ref, bu_ref, st_ref):
    c = pl.program_id(1)
    tc = x_ref.shape[0]

    @pl.when(c == 0)
    def _():
        st_ref[...] = jnp.zeros_like(st_ref)

    u = _rms(x_ref[...], g_ref[...])
    _s5_input_states(u.astype(BF16), bw_ref, bu_ref)
    for k in range(S5_W // S5_LANE_CHUNK):
        lr = pl.ds(k * S5_LANE_CHUNK, S5_LANE_CHUNK)
        li = pl.ds(S5_W + k * S5_LANE_CHUNK, S5_LANE_CHUNK)
        ar = ab_ref[0:1, lr]
        ai = ab_ref[1:2, lr]

        def step(t, carry):
            hr, hi = carry
            row = pl.ds(t, 1)
            nr = ar * hr - ai * hi + bu_ref[row, lr]
            ni = ar * hi + ai * hr + bu_ref[row, li]
            bu_ref[row, lr] = nr
            bu_ref[row, li] = ni
            return nr, ni

        hr, hi = lax.fori_loop(0, tc, step, (st_ref[0:1, lr], st_ref[1:2, lr]), unroll=8)
        st_ref[0:1, lr] = hr
        st_ref[1:2, lr] = hi
    def store(cols, val):
        z_ref[:, cols] = val

    _s5_output(u, bu_ref, cre_ref, cim_ref, d_ref, store)
    hre_ref[0] = st_ref[0:1, :]
    him_ref[0] = st_ref[1:2, :]


def _s5_prompt(x, g, ab, bw, cre, cim, d, B, S):
    tc = min(S5_CHUNK_ROWS, S)
    nc = S // tc
    st = jax.ShapeDtypeStruct((B, 1, S5_W), F32)
    return pl.pallas_call(
        _s5_prompt_kernel,
        grid=(B, nc),
        in_specs=[pl.BlockSpec((tc, D_MODEL), lambda b, c: (b * nc + c, 0)), _full((1, D_MODEL)),
                  _full(ab.shape), _full(bw.shape), _full(cre.shape), _full(cim.shape), _full((1, D_MODEL))],
        out_specs=[pl.BlockSpec((tc, D_MODEL), lambda b, c: (b * nc + c, 0)),
                   pl.BlockSpec((1, 1, S5_W), lambda b, c: (b, 0, 0)),
                   pl.BlockSpec((1, 1, S5_W), lambda b, c: (b, 0, 0))],
        out_shape=[jax.ShapeDtypeStruct((B * S, D_MODEL), BF16), st, st],
        scratch_shapes=[pltpu.VMEM((tc, 2 * S5_W), F32), pltpu.VMEM((2, S5_W), F32)],
        compiler_params=_params(("parallel", "arbitrary")),
        name="s5_prompt",
    )(x, g, ab, bw, cre, cim, d)


def _s5_sample_kernel(x_ref, g_ref, ab_ref, bw_ref, cre_ref, cim_ref, d_ref, h0re_ref, h0im_ref,
                      z_ref, hre_ref, him_ref, bu_ref):
    tn, ns, _ = x_ref.shape
    u = _rms(x_ref[...].reshape(tn * ns, D_MODEL), g_ref[...])
    _s5_input_states(u.astype(BF16), bw_ref, bu_ref)
    ar = ab_ref[0:1, 0:S5_W]
    ai = ab_ref[1:2, 0:S5_W]
    hre_ref[...] = h0re_ref[...]
    him_ref[...] = h0im_ref[...]
    for j in range(tn):
        rows = slice(j * ns, (j + 1) * ns)
        hr = hre_ref[...]
        hi = him_ref[...]
        nr = ar * hr - ai * hi + bu_ref[rows, 0:S5_W]
        ni = ar * hi + ai * hr + bu_ref[rows, S5_W:2 * S5_W]
        hre_ref[...] = nr
        him_ref[...] = ni
        bu_ref[rows, 0:S5_W] = nr
        bu_ref[rows, S5_W:2 * S5_W] = ni

    def store(cols, val):
        z_ref[:, :, cols] = val.reshape(tn, ns, S5_BLK_CH)

    _s5_output(u, bu_ref, cre_ref, cim_ref, d_ref, store)


def _s5_sample(x_t, g, ab, bw, cre, cim, d, h0re, h0im):
    tn, NB, _ = x_t.shape
    ns = min(S5_SEQ_GROUP, NB)
    st = jax.ShapeDtypeStruct((NB, S5_W), F32)
    stb = pl.BlockSpec((ns, S5_W), lambda i: (i, 0))
    xb = pl.BlockSpec((tn, ns, D_MODEL), lambda i: (0, i, 0))
    return pl.pallas_call(
        _s5_sample_kernel,
        grid=(NB // ns,),
        in_specs=[xb, _full((1, D_MODEL)),
                  _full(ab.shape), _full(bw.shape), _full(cre.shape), _full(cim.shape), _full((1, D_MODEL)),
                  stb, stb],
        out_specs=[xb, stb, stb],
        out_shape=[jax.ShapeDtypeStruct((tn, NB, D_MODEL), BF16), st, st],
        scratch_shapes=[pltpu.VMEM((tn * ns, 2 * S5_W), F32)],
        compiler_params=_params(("parallel",)),
        name="s5_sample",
    )(x_t, g, ab, bw, cre, cim, d, h0re, h0im)


def _rope_tables(pos):
    half = ROPE_DIM // 2
    inv_freq = ROPE_THETA ** (-jnp.arange(half, dtype=F32) * (2.0 / ROPE_DIM))
    ang = pos.astype(F32)[:, None] * inv_freq[None, :]
    c, s = jnp.cos(ang), jnp.sin(ang)
    n = pos.shape[0]
    pad = HEAD_DIM - ROPE_DIM
    c64 = jnp.concatenate([c, c, jnp.ones((n, pad), F32)], axis=-1)
    s64 = jnp.concatenate([-s, s, jnp.zeros((n, pad), F32)], axis=-1)
    return jnp.tile(c64, (1, LANES // HEAD_DIM)), jnp.tile(s64, (1, LANES // HEAD_DIM))


def _w_in_layout(w):
    o = Q_W + KV_W
    gate = jnp.pad(w[:, o:o + GATE_W], ((0, 0), (0, GATE_PAD - GATE_W)))
    return jnp.concatenate([w[:, :o], w[:, o + GATE_W:], gate], axis=1).astype(BF16)


def _compress_layout(pe, w1, w2):
    eye = jnp.eye(2, dtype=F32)
    w1r = w1.reshape(2, 2, CMP_STRIDE, HEAD_DIM, HEAD_DIM)
    w1bd = jnp.einsum('cmsde,ca,kb->msckdabe', w1r, eye, eye)
    w1bd = w1bd.reshape(2 * CMP_STRIDE, 2 * LANES, 2 * LANES).astype(BF16)
    w2bd = jnp.einsum('cde,ca,kb->ckdabe', w2, eye, eye).reshape(2 * LANES, 2 * LANES).astype(BF16)
    per = pe.reshape(2, 2, CMP_STRIDE, HEAD_DIM)
    pel = jnp.broadcast_to(jnp.transpose(per, (1, 2, 0, 3))[:, :, :, None, :],
                           (2, CMP_STRIDE, 2, N_KV, HEAD_DIM)).reshape(2 * CMP_STRIDE, 2 * LANES)
    return pel, w1bd, w2bd


def _s5_layout(bb_re, bb_im, c_re, c_im):
    eye = jnp.eye(S5_GROUPS // S5_BLOCKS, dtype=F32)
    gl = S5_GROUPS // S5_BLOCKS
    bb = jnp.stack([bb_re, bb_im]).reshape(2, S5_BLOCKS, gl, S5_STATE, S5_GROUP)
    bw = jnp.einsum('rkgpc,gh->kgcrhp', bb, eye).reshape(S5_BLOCKS, S5_BLK_CH, 2 * S5_BLK_ST).astype(BF16)

    def cl(c):
        c = c.reshape(S5_BLOCKS, gl, S5_GROUP, S5_STATE)
        return jnp.einsum('kgcp,gh->kgphc', c, eye).reshape(S5_BLOCKS, S5_BLK_ST, S5_BLK_CH).astype(BF16)

    return bw, cl(c_re), cl(c_im)


def _block_onehot(n_keys):
    blk = jnp.arange(LANES)[:, None]
    key = jnp.arange(n_keys)[None, :]
    return (key // SLC_BLOCK == blk).astype(BF16)


def kernel(x_prompt, x_sample, cache_nsa_kv, cache_nsa_win, state_s5_re, state_s5_im, page_table, norm_mix, norm_ffn, norm_final, w_in, w_out, cmp_pe, cmp_w1, cmp_w2, sgu_ln_g, sgu_ln_b, sgu_w, sgu_b, s5_a_re, s5_a_im, s5_log_step, s5_b_re, s5_b_im, s5_c_re, s5_c_im, s5_d, glu_w_a, glu_w_b, ffn_w1, ffn_w3, ffn_w2):
    B, S, _ = x_prompt.shape
    NB, tn, _ = x_sample.shape
    n_even, pool, page = cache_nsa_kv.shape[:3]
    n_pages = page_table.shape[1]
    past_len = n_pages * page
    wb = cache_nsa_win.shape[2]
    assert S % KEY_TILE == 0 and S // SLC_BLOCK <= LANES and S >= WINDOW + Q_BLOCK
    assert past_len % SLC_BLOCK == 0 and tn <= CMP_STRIDE and SGU_CHUNK % tn == 0
    assert NB % NSA_SEQ_GROUP == 0 and -(-(past_len + tn) // SLC_BLOCK) <= LANES

    xp = x_prompt.reshape(B * S, D_MODEL)
    xs = x_sample.reshape(NB * tn, D_MODEL)
    cache = cache_nsa_kv.reshape(n_even * pool, page, 4 * LANES)
    win = cache_nsa_win.reshape(n_even * NB, wb, 2 * LANES)
    pt_flat = page_table.reshape(-1).astype(jnp.int32)
    cos_p, sin_p = _rope_tables(jnp.arange(S))
    cos_s, sin_s = _rope_tables(jnp.tile(past_len + jnp.arange(tn), NB))
    e_prompt = _block_onehot(S)
    e_sample = _block_onehot(past_len)
    ncb_s = past_len // CMP_STRIDE
    tm_p = min(TOKEN_TILE, S)
    tm_s = min(TOKEN_TILE, NB * tn)
    row = lambda a: a.reshape(1, -1)

    kv_p, kv_s, win_p, win_s, sgu_v_s = [], [], [], [], []
    s5p_re, s5p_im, s5s_re, s5s_im = [], [], [], []
    for layer in range(DEPTH):
        final = layer == DEPTH - 1
        ffn = (row(norm_ffn[layer]), ffn_w1[layer].astype(BF16), ffn_w3[layer].astype(BF16),
               ffn_w2[layer].astype(BF16), row(norm_final))
        gmix = row(norm_mix[layer])
        if layer % 2 == 0:
            e = layer // 2
            wi = _w_in_layout(w_in[e])
            lng, lnb = row(sgu_ln_g[e]), row(sgu_ln_b[e])
            pel, w1bd, w2bd = _compress_layout(cmp_pe[e], cmp_w1[e], cmp_w2[e])
            qp, kvp, kvbp, gp, up, vnp = _inproj(xp, gmix, wi, cos_p, sin_p, lng, lnb, tm_p)
            qs, kvs, _, gs, us, vns = _inproj(xs, gmix, wi, cos_s, sin_s, lng, lnb, tm_s)
            ckp, cvp = _compress_prompt(kvp, B, S, pel, w1bd, w2bd)
            ap = _nsa_prompt(qp, gp, ckp, cvp, kvbp, e_prompt, B, S)
            cks, cvs = _compress_sample(cache, pt_flat, NB, n_pages, page, e * pool, pel, w1bd, w2bd)
            selneg, ocg = _nsa_sample_select(qs, gs, cks, cvs, NB, tn, past_len, ncb_s)
            a_s = _nsa_sample_attend(cache, win, pt_flat, qs, gs, selneg, ocg, kvs, e_sample,
                                     NB, tn, n_pages, page, e * pool, e * NB)
            wo = w_out[e].astype(BF16)
            bmix_p = sgu_b[e][:, :, None]
            reps = SGU_CHUNK // tn
            wmix_s = jnp.tile(sgu_w[e][:, :tn, :tn], (1, reps, reps))
            bmix_s = jnp.tile(sgu_b[e][:, :tn], (1, reps))[:, :, None]
            xp = _even_tail(xp, ap, up, vnp, sgu_w[e], bmix_p, wo, *ffn, chunk=SGU_CHUNK, final=final)
            xs = _even_tail(xs, a_s, us, vns, wmix_s, bmix_s, wo, *ffn, chunk=tn, final=final)
            kvp5 = kvp.reshape(B, S, 6, N_KV, HEAD_DIM)
            kvs5 = kvs.reshape(NB, tn, 6, N_KV, HEAD_DIM)
            kv_p.append(kvp5[:, :, 0:4])
            kv_s.append(kvs5[:, :, 0:4])
            win_p.append(kvp5[:, S - min(WINDOW, S):, 4:6])
            win_s.append(kvs5[:, :, 4:6])
            sgu_v_s.append(vns.reshape(NB, tn, SGU_GROUPS, SGU_DIM))
        else:
            o = layer // 2
            ab_re, ab_im, bb_re, bb_im = _s5_disc(s5_a_re[o], s5_a_im[o], s5_log_step[o], s5_b_re[o], s5_b_im[o])
            ab = jnp.concatenate([ab_re.reshape(1, S5_W), ab_im.reshape(1, S5_W)], axis=0)
            bw, cre, cim = _s5_layout(bb_re.reshape(S5_GROUPS, S5_STATE, S5_GROUP),
                                      bb_im.reshape(S5_GROUPS, S5_STATE, S5_GROUP), s5_c_re[o], s5_c_im[o])
            d = row(s5_d[o])
            zp, hpr, hpi = _s5_prompt(xp, gmix, ab, bw, cre, cim, d, B, S)
            xs_t = jnp.transpose(xs.reshape(NB, tn, D_MODEL), (1, 0, 2))
            zs_t, hsr, hsi = _s5_sample(xs_t, gmix, ab, bw, cre, cim, d,
                                        state_s5_re[o].reshape(NB, S5_W), state_s5_im[o].reshape(NB, S5_W))
            zs = jnp.transpose(zs_t, (1, 0, 2)).reshape(NB * tn, D_MODEL)
            wa, wb_ = glu_w_a[o].astype(BF16), glu_w_b[o].astype(BF16)
            xp = _odd_tail(xp, zp, wa, wb_, *ffn, final=final)
            xs = _odd_tail(xs, zs, wa, wb_, *ffn, final=final)
            s5p_re.append(hpr.reshape(B, S5_GROUPS, S5_STATE))
            s5p_im.append(hpi.reshape(B, S5_GROUPS, S5_STATE))
            s5s_re.append(hsr.reshape(NB, S5_GROUPS, S5_STATE))
            s5s_im.append(hsi.reshape(NB, S5_GROUPS, S5_STATE))
    return (xp.reshape(B, S, D_MODEL), xs.reshape(NB, tn, D_MODEL), jnp.stack(kv_p), jnp.stack(kv_s),
            jnp.stack(win_p), jnp.stack(win_s), jnp.stack(sgu_v_s), jnp.stack(s5p_re), jnp.stack(s5p_im),
            jnp.stack(s5s_re), jnp.stack(s5s_im))
```

```python
import functools
import math

import jax
import jax.numpy as jnp
from jax import lax
from jax.experimental import pallas as pl
from jax.experimental.pallas import tpu as pltpu

F32 = jnp.float32
BF16 = jnp.bfloat16

D_MODEL = 1024
DEPTH = 4
N_HEADS = 8
N_KV = 2
GQ = N_HEADS // N_KV
HEAD_DIM = 64
ROPE_DIM = 16
ROPE_THETA = 500000.0
CMP_BLOCK = 32
CMP_STRIDE = 16
SLC_BLOCK = 64
N_SELECT = 16
WINDOW = 512
Q_BLOCK = 128
SGU_GROUPS = 4
SGU_DIM = 128
SGU_CHUNK = 128
Q_W = N_HEADS * HEAD_DIM
KV_W = 6 * N_KV * HEAD_DIM
GATE_W = 3 * N_HEADS
U_W = SGU_GROUPS * SGU_DIM
S5_GROUP = 16
S5_GROUPS = D_MODEL // S5_GROUP
S5_STATE = 64
S5_W = S5_GROUPS * S5_STATE
D_FF = 2816
EPS = 1e-6
NEG_INF = -1e30
TINY = 1e-30
FORCE_SCORE = 1e4

LANES = 128
GATE_PAD = LANES
W_IN_COLS = Q_W + KV_W + 2 * U_W + GATE_PAD
FF_CHUNK = 256
TOKEN_TILE = 512
KEY_TILE = 512
S5_CHUNK_ROWS = 256
S5_SEQ_GROUP = 32
S5_LANE_CHUNK = 1024
NSA_SEQ_GROUP = 16
VMEM_LIMIT = 56 * 1024 * 1024


def _dot(a, b):
    return jnp.dot(a, b, preferred_element_type=F32)


def _dot_nt(a, b):
    return lax.dot_general(a, b, (((1,), (1,)), ((), ())), preferred_element_type=F32)


def _rms(x, g):
    return x * lax.rsqrt(jnp.mean(x * x, axis=-1, keepdims=True) + EPS) * g


def _sigmoid(x):
    return 1.0 / (1.0 + jnp.exp(-x))


def _params(sem):
    return pltpu.CompilerParams(dimension_semantics=sem, vmem_limit_bytes=VMEM_LIMIT)


def _full(shape):
    n = len(shape)
    return pl.BlockSpec(shape, lambda *_: (0,) * n)


def _inproj_kernel(x_ref, g_ref, w_ref, cos_ref, sin_ref, lng_ref, lnb_ref,
                   q_ref, kv_ref, kvb_ref, gate_ref, u_ref, vn_ref):
    x = x_ref[...]
    h = _rms(x, g_ref[...]).astype(BF16)
    cos = cos_ref[...]
    sin = sin_ref[...]
    lane = lax.broadcasted_iota(jnp.int32, cos.shape, 1)
    low = (lane % HEAD_DIM) < (ROPE_DIM // 2)

    def rope(t):
        rot = jnp.where(low, pltpu.roll(t, LANES - ROPE_DIM // 2, 1), pltpu.roll(t, ROPE_DIM // 2, 1))
        return t * cos + rot * sin

    pq = _dot(h, w_ref[:, 0:Q_W])
    for j in range(Q_W // LANES):
        q_ref[:, j * LANES:(j + 1) * LANES] = rope(pq[:, j * LANES:(j + 1) * LANES]) * (HEAD_DIM ** -0.5)
    pkv = _dot(h, w_ref[:, Q_W:Q_W + KV_W])
    for j in range(KV_W // LANES):
        t = pkv[:, j * LANES:(j + 1) * LANES]
        if j % 2 == 0:
            t = rope(t)
        kv_ref[:, j * LANES:(j + 1) * LANES] = t
        if j >= 2:
            kvb_ref[:, (j - 2) * LANES:(j - 1) * LANES] = t.astype(BF16)
    o = Q_W + KV_W
    u_ref[...] = jax.nn.gelu(_dot(h, w_ref[:, o:o + U_W]))
    v = jax.nn.gelu(_dot(h, w_ref[:, o + U_W:o + 2 * U_W]))
    vc = v - jnp.mean(v, axis=-1, keepdims=True)
    var = jnp.mean(vc * vc, axis=-1, keepdims=True)
    vn_ref[...] = vc * lax.rsqrt(var + EPS) * lng_ref[...] + lnb_ref[...]
    gate_ref[...] = _sigmoid(_dot(h, w_ref[:, o + 2 * U_W:o + 2 * U_W + GATE_PAD]))


def _inproj(x, g, w, cos, sin, lng, lnb, tm):
    T = x.shape[0]
    nt = T // tm
    ntab = cos.shape[0] // tm
    row = lambda w_: pl.BlockSpec((tm, w_), lambda i: (i, 0))
    tab = pl.BlockSpec((tm, LANES), lambda i: (i % ntab, 0))
    return pl.pallas_call(
        _inproj_kernel,
        grid=(nt,),
        in_specs=[row(D_MODEL), _full((1, D_MODEL)), _full((D_MODEL, W_IN_COLS)), tab, tab,
                  _full((1, U_W)), _full((1, U_W))],
        out_specs=[row(Q_W), row(KV_W), row(4 * LANES), row(GATE_PAD), row(U_W), row(U_W)],
        out_shape=[jax.ShapeDtypeStruct((T, Q_W), F32), jax.ShapeDtypeStruct((T, KV_W), F32),
                   jax.ShapeDtypeStruct((T, 4 * LANES), BF16), jax.ShapeDtypeStruct((T, GATE_PAD), F32),
                   jax.ShapeDtypeStruct((T, U_W), F32), jax.ShapeDtypeStruct((T, U_W), F32)],
        compiler_params=_params(("parallel",)),
        name="inproj",
    )(x, g, w, cos, sin, lng, lnb)


def _compress_core(load_rows, nch, pe_ref, w1_ref, w2_ref, acc0_ref, acc1_ref, k_ref, v_ref):
    for s in range(CMP_STRIDE):
        xs = load_rows(s)
        for m, acc in ((0, acc0_ref), (1, acc1_ref)):
            idx = m * CMP_STRIDE + s
            part = _dot((xs + pe_ref[idx:idx + 1, :]).astype(BF16), w1_ref[idx])
            if s == 0:
                acc[...] = part
            else:
                acc[...] += part
    hid = acc0_ref[...] + pltpu.roll(acc1_ref[...], nch - 1, 0)
    out = _dot(jax.nn.gelu(hid).astype(BF16), w2_ref[...])
    k_ref[...] = out[:, 0:LANES].astype(BF16).reshape(k_ref.shape)
    v_ref[...] = out[:, LANES:2 * LANES].astype(BF16).reshape(v_ref.shape)


def _compress_prompt_kernel(xk_ref, xv_ref, pe_ref, w1_ref, w2_ref, k_ref, v_ref, acc0_ref, acc1_ref):
    nch = xk_ref.shape[0] // CMP_STRIDE
    rows = lambda s: pl.ds(s, nch, stride=CMP_STRIDE)
    load = lambda s: jnp.concatenate([xk_ref[rows(s), :], xv_ref[rows(s), :]], axis=1)
    _compress_core(load, nch, pe_ref, w1_ref, w2_ref, acc0_ref, acc1_ref, k_ref, v_ref)


def _compress_prompt(kv, B, S, pe, w1, w2):
    nch = S // CMP_STRIDE
    out = jax.ShapeDtypeStruct((B, nch, LANES), BF16)
    return pl.pallas_call(
        _compress_prompt_kernel,
        grid=(B,),
        in_specs=[pl.BlockSpec((S, LANES), lambda b: (b, 0)), pl.BlockSpec((S, LANES), lambda b: (b, 1)),
                  _full(pe.shape), _full(w1.shape), _full(w2.shape)],
        out_specs=[pl.BlockSpec((1, nch, LANES), lambda b: (b, 0, 0))] * 2,
        out_shape=[out, out],
        scratch_shapes=[pltpu.VMEM((nch, 2 * LANES), F32)] * 2,
        compiler_params=_params(("parallel",)),
        name="compress_prompt",
    )(kv, kv, pe, w1, w2)


def _compress_sample_kernel(n_pages, pt_ref, *refs):
    pages = refs[:n_pages]
    pe_ref, w1_ref, w2_ref, k_ref, v_ref, acc0_ref, acc1_ref, xk_ref, xv_ref = refs[n_pages:]
    page = pages[0].shape[2]
    nch = n_pages * page // CMP_STRIDE
    for j, p in enumerate(pages):
        xt = p[0].T
        xk_ref[j * page:(j + 1) * page, :] = xt[:, 0:LANES]
        xv_ref[j * page:(j + 1) * page, :] = xt[:, LANES:2 * LANES]
    rows = lambda s: pl.ds(s, nch, stride=CMP_STRIDE)
    load = lambda s: jnp.concatenate([xk_ref[rows(s), :], xv_ref[rows(s), :]], axis=1)
    _compress_core(load, nch, pe_ref, w1_ref, w2_ref, acc0_ref, acc1_ref, k_ref, v_ref)


def _page_specs(n_pages, page, base, row_block):
    def spec(j):
        return pl.BlockSpec((1, 2 * LANES, page),
                            lambda n, pt: (base + pt[n * n_pages + j], row_block, 0))
    return [spec(j) for j in range(n_pages)]


def _compress_sample(cache_t, pt_flat, NB, n_pages, page, base, pe, w1, w2):
    nch = n_pages * page // CMP_STRIDE
    out = jax.ShapeDtypeStruct((NB * nch, LANES), BF16)
    cfull = lambda shape: pl.BlockSpec(shape, lambda n, pt: (0,) * len(shape))
    grid_spec = pltpu.PrefetchScalarGridSpec(
        num_scalar_prefetch=1,
        grid=(NB,),
        in_specs=_page_specs(n_pages, page, base, 0) + [cfull(pe.shape), cfull(w1.shape), cfull(w2.shape)],
        out_specs=[pl.BlockSpec((nch, LANES), lambda n, pt: (n, 0))] * 2,
        scratch_shapes=[pltpu.VMEM((nch, 2 * LANES), F32)] * 2 + [pltpu.VMEM((n_pages * page, LANES), F32)] * 2,
    )
    return pl.pallas_call(
        functools.partial(_compress_sample_kernel, n_pages),
        grid_spec=grid_spec,
        out_shape=[out, out],
        compiler_params=_params(("parallel",)),
        name="compress_sample",
    )(pt_flat, *([cache_t] * n_pages), pe, w1, w2)


def _group_queries(q, kv):
    lane = lax.broadcasted_iota(jnp.int32, (q.shape[0], LANES), 1)
    mine = (lane < HEAD_DIM) if kv == 0 else (lane >= HEAD_DIM)
    parts = []
    for g in range(GQ):
        h = kv * GQ + g
        t = q[:, (h // 2) * LANES:(h // 2 + 1) * LANES]
        if h % 2 != kv:
            t = pltpu.roll(t, HEAD_DIM, 1)
        parts.append(jnp.where(mine, t, 0.0))
    return jnp.concatenate(parts, axis=0).astype(BF16)


def _softmax_rows(s, mask):
    s = jnp.where(mask, s, NEG_INF)
    m = jnp.max(s, axis=-1, keepdims=True)
    e = jnp.where(mask, jnp.exp(s - m), 0.0)
    return e / jnp.maximum(jnp.sum(e, axis=-1, keepdims=True), TINY)


def _select_blocks(score):
    st = score.T
    blk = lax.broadcasted_iota(jnp.int32, st.shape, 0).astype(F32)

    def body(_, carry):
        sc, sel = carry
        m = jnp.max(sc, axis=0, keepdims=True)
        first = jnp.min(jnp.where(sc == m, blk, float(LANES)), axis=0, keepdims=True)
        hit = blk == first
        return jnp.where(hit, -2.0, sc), jnp.where(hit, 1.0, sel)

    _, sel = lax.fori_loop(0, N_SELECT, body, (st, jnp.zeros_like(st)))
    return sel.T


def _block_scores(imp, tq, n_slc):
    blk = lax.broadcasted_iota(jnp.int32, imp.shape, 1)
    cur = tq // SLC_BLOCK
    forced = (blk == 0) | (blk == cur) | (blk == cur - 1)
    allowed = blk * SLC_BLOCK <= tq
    score = jnp.where(forced, FORCE_SCORE, jnp.where(allowed, imp, -1.0))
    return jnp.where(blk < n_slc, score, -3.0)


def _cover(ci, sj):
    return ((ci * CMP_STRIDE < (sj + 1) * SLC_BLOCK) & (ci * CMP_STRIDE + CMP_BLOCK > sj * SLC_BLOCK))


def _nsa_prompt_kernel(q_ref, gate_ref, ck_ref, cv_ref, kvb_ref, et_ref, out_ref, *, seq):
    i = pl.program_id(1)
    R = Q_BLOCK
    s0 = i * R
    q = q_ref[...]
    gate = gate_ref[...]
    ncb = ck_ref.shape[1]
    n_cmp = seq // CMP_STRIDE - 1
    n_slc = seq // SLC_BLOCK
    tok4 = lax.broadcasted_iota(jnp.int32, (GQ * R, 1), 0) % R
    tq4 = s0 + tok4
    tq = s0 + lax.broadcasted_iota(jnp.int32, (R, 1), 0)
    ck = ck_ref[0]
    cv = cv_ref[0]
    ci = lax.broadcasted_iota(jnp.int32, (ncb, LANES), 0)
    sj = lax.broadcasted_iota(jnp.int32, (ncb, LANES), 1)
    cover = jnp.where(_cover(ci, sj) & (ci < n_cmp) & (sj < n_slc), 1.0, 0.0).astype(BF16)
    cmp_i = lax.broadcasted_iota(jnp.int32, (GQ * R, ncb), 1)
    cmp_mask = (cmp_i * CMP_STRIDE + (CMP_BLOCK - 1) <= tq4) & (cmp_i < n_cmp)
    n_tiles = (s0 + R + KEY_TILE - 1) // KEY_TILE
    wstart = pl.multiple_of(jnp.maximum(s0 - WINDOW, 0), R)
    wlen = WINDOW + R
    wpos = wstart + lax.broadcasted_iota(jnp.int32, (GQ * R, wlen), 1)
    wdiff = tq4 - wpos
    wmask = (wdiff >= 0) & (wdiff < WINDOW)
    kcol = lax.broadcasted_iota(jnp.int32, (GQ * R, KEY_TILE), 1)

    qzs, o_cs, scores = [], [], []
    for kv in range(N_KV):
        qz = _group_queries(q, kv)
        p_c = _softmax_rows(_dot_nt(qz, ck), cmp_mask).astype(BF16)
        o_cs.append(_dot(p_c, cv))
        imp4 = _dot(p_c, cover)
        imp = imp4[0:R] + imp4[R:2 * R] + imp4[2 * R:3 * R] + imp4[3 * R:4 * R]
        scores.append(_block_scores(imp, tq, n_slc))
        qzs.append(qz)
    sel = _select_blocks(jnp.concatenate(scores, axis=0))

    q_augs = []
    for kv in range(N_KV):
        selneg = jnp.where(sel[kv * R:(kv + 1) * R] > 0.5, 0.0, NEG_INF).astype(BF16)
        q_augs.append(jnp.concatenate([qzs[kv], jnp.concatenate([selneg] * GQ, axis=0)], axis=1))

    def tile_step(off, carry, diagonal):
        k_aug = jnp.concatenate([kvb_ref[pl.ds(off, KEY_TILE), 0:LANES],
                                 et_ref[pl.ds(off, KEY_TILE), :]], axis=1)
        vt = kvb_ref[pl.ds(off, KEY_TILE), LANES:2 * LANES]
        out = []
        for kv in range(N_KV):
            m, l, acc = carry[kv]
            s = _dot_nt(q_augs[kv], k_aug)
            if diagonal:
                s = jnp.where(off + kcol <= tq4, s, NEG_INF)
            m_new = jnp.maximum(m, jnp.max(s, axis=-1, keepdims=True))
            alpha = jnp.exp(m - m_new)
            p = jnp.exp(s - m_new)
            l = alpha * l + jnp.sum(p, axis=-1, keepdims=True)
            acc = alpha * acc + _dot(p.astype(BF16), vt)
            out.append((m_new, l, acc))
        return tuple(out)

    init1 = (jnp.full((GQ * R, 1), NEG_INF, F32), jnp.zeros((GQ * R, 1), F32),
             jnp.zeros((GQ * R, LANES), F32))
    carry = lax.fori_loop(
        0, n_tiles - 1, lambda t, c: tile_step(pl.multiple_of(t * KEY_TILE, KEY_TILE), c, False),
        (init1,) * N_KV)
    carry = tile_step(pl.multiple_of((n_tiles - 1) * KEY_TILE, KEY_TILE), carry, True)

    for kv in range(N_KV):
        qz, o_c = qzs[kv], o_cs[kv]
        _, l, acc = carry[kv]
        o_s = acc / jnp.maximum(l, TINY)

        kw = kvb_ref[pl.ds(wstart, wlen), 2 * LANES:3 * LANES]
        vw = kvb_ref[pl.ds(wstart, wlen), 3 * LANES:4 * LANES]
        p_w = _softmax_rows(_dot_nt(qz, kw), wmask).astype(BF16)
        o_w = _dot(p_w, vw)

        for g in range(GQ):
            h = kv * GQ + g
            rows = slice(g * R, (g + 1) * R)
            lanes = slice(kv * HEAD_DIM, (kv + 1) * HEAD_DIM)
            o = (gate[:, 3 * h:3 * h + 1] * o_c[rows, lanes]
                 + gate[:, 3 * h + 1:3 * h + 2] * o_s[rows, lanes]
                 + gate[:, 3 * h + 2:3 * h + 3] * o_w[rows, lanes])
            out_ref[:, h * HEAD_DIM:(h + 1) * HEAD_DIM] = o


def _nsa_prompt(q, gate, ck, cv, kvb, emat, B, S):
    nq = S // Q_BLOCK
    ncb = ck.shape[1]
    rowblk = lambda w_: pl.BlockSpec((Q_BLOCK, w_), lambda b, i: (b * nq + i, 0))
    return pl.pallas_call(
        functools.partial(_nsa_prompt_kernel, seq=S),
        grid=(B, nq),
        in_specs=[rowblk(Q_W), rowblk(GATE_PAD),
                  pl.BlockSpec((1, ncb, LANES), lambda b, i: (b, 0, 0)),
                  pl.BlockSpec((1, ncb, LANES), lambda b, i: (b, 0, 0)),
                  pl.BlockSpec((S, 4 * LANES), lambda b, i: (b, 0)),
                  pl.BlockSpec(emat.shape, lambda b, i: (0, 0))],
        out_specs=rowblk(Q_W),
        out_shape=jax.ShapeDtypeStruct((B * S, Q_W), F32),
        compiler_params=_params(("parallel", "arbitrary")),
        name="nsa_prompt",
    )(q, gate, ck, cv, kvb, emat)


def _nsa_sample_select_kernel(q_ref, gate_ref, ck_ref, cv_ref, sel_ref, oc_ref, *, past_len, tn, ncb):
    R = q_ref.shape[0]
    q = q_ref[...]
    gate = gate_ref[...]
    total = past_len + tn
    n_cmp = total // CMP_STRIDE - 1
    n_slc = -(-total // SLC_BLOCK)
    ncol = ck_ref.shape[0]
    row4 = lax.broadcasted_iota(jnp.int32, (GQ * R, 1), 0) % R
    tq4 = past_len + row4 % tn
    tq = past_len + lax.broadcasted_iota(jnp.int32, (R, 1), 0) % tn
    col = lax.broadcasted_iota(jnp.int32, (GQ * R, ncol), 1)
    ci = col % ncb
    cmp_mask = ((col // ncb == row4 // tn) & (ci * CMP_STRIDE + (CMP_BLOCK - 1) <= tq4) & (ci < n_cmp))
    cr = lax.broadcasted_iota(jnp.int32, (ncol, LANES), 0) % ncb
    sj = lax.broadcasted_iota(jnp.int32, (ncol, LANES), 1)
    cover = jnp.where(_cover(cr, sj) & (cr < n_cmp) & (sj < n_slc), 1.0, 0.0).astype(BF16)
    ck = ck_ref[...]
    cv = cv_ref[...]
    scores = []
    for kv in range(N_KV):
        qz = _group_queries(q, kv)
        p_c = _softmax_rows(_dot_nt(qz, ck), cmp_mask).astype(BF16)
        o_c = _dot(p_c, cv)
        imp4 = _dot(p_c, cover)
        imp = imp4[0:R] + imp4[R:2 * R] + imp4[2 * R:3 * R] + imp4[3 * R:4 * R]
        scores.append(_block_scores(imp, tq, n_slc))
        for g in range(GQ):
            h = kv * GQ + g
            oc_ref[:, h * HEAD_DIM:(h + 1) * HEAD_DIM] = (
                gate[:, 3 * h:3 * h + 1] * o_c[g * R:(g + 1) * R, kv * HEAD_DIM:(kv + 1) * HEAD_DIM])
    sel = _select_blocks(jnp.concatenate(scores, axis=0))
    for kv in range(N_KV):
        sel_ref[:, kv * LANES:(kv + 1) * LANES] = jnp.where(sel[kv * R:(kv + 1) * R] > 0.5, 0.0, NEG_INF)


def _nsa_sample_select(q, gate, ck, cv, NB, tn, past_len, ncb):
    R = NSA_SEQ_GROUP * tn
    rowblk = lambda w_: pl.BlockSpec((R, w_), lambda i: (i, 0))
    cblk = pl.BlockSpec((NSA_SEQ_GROUP * ncb, LANES), lambda i: (i, 0))
    return pl.pallas_call(
        functools.partial(_nsa_sample_select_kernel, past_len=past_len, tn=tn, ncb=ncb),
        grid=(NB // NSA_SEQ_GROUP,),
        in_specs=[rowblk(Q_W), rowblk(GATE_PAD), cblk, cblk],
        out_specs=[rowblk(2 * LANES), rowblk(Q_W)],
        out_shape=[jax.ShapeDtypeStruct((NB * tn, 2 * LANES), F32),
                   jax.ShapeDtypeStruct((NB * tn, Q_W), F32)],
        compiler_params=_params(("parallel",)),
        name="nsa_sample_select",
    )(q, gate, ck, cv)


def _nsa_sample_attend_kernel(n_pages, pt_ref, *refs, past_len, tn):
    pages = refs[:n_pages]
    q_ref, gate_ref, sel_ref, oc_ref, kvn_ref, win_ref, e_ref, out_ref = refs[n_pages:]
    R = tn
    q = q_ref[...]
    gate = gate_ref[...]
    kvn = kvn_ref[...]
    wb = win_ref.shape[2]
    new_blk = past_len // SLC_BLOCK
    tok4 = lax.broadcasted_iota(jnp.int32, (GQ * R, 1), 0) % R
    newcol = lax.broadcasted_iota(jnp.int32, (GQ * R, LANES), 1)
    causal_new = newcol <= tok4
    kt_past = jnp.concatenate([p[0, 0:LANES, :] for p in pages], axis=1).astype(BF16)
    vt_past = jnp.concatenate([p[0, LANES:2 * LANES, :] for p in pages], axis=1).astype(BF16)
    pad = jnp.zeros((LANES - tn, KV_W), F32)
    kvn = jnp.concatenate([kvn, pad], axis=0)
    k_new = kvn[:, 2 * LANES:3 * LANES].astype(BF16)
    v_new = kvn[:, 3 * LANES:4 * LANES].astype(BF16)
    kwt = win_ref[0, 0:LANES, :].astype(BF16)
    vwt = win_ref[0, LANES:2 * LANES, :].astype(BF16)
    kw_new = kvn[:, 4 * LANES:5 * LANES].astype(BF16)
    vw_new = kvn[:, 5 * LANES:6 * LANES].astype(BF16)
    wcol = lax.broadcasted_iota(jnp.int32, (GQ * R, wb), 1)
    wdiff = wb + tok4 - wcol
    wmask = (wdiff >= 0) & (wdiff < WINDOW) & (wcol >= wb - past_len)

    for kv in range(N_KV):
        qz = _group_queries(q, kv)
        sel4 = jnp.concatenate([sel_ref[:, kv * LANES:(kv + 1) * LANES]] * GQ, axis=0)
        s = _dot(qz, kt_past) + _dot(sel4.astype(BF16), e_ref[...])
        s_new = jnp.where(causal_new, _dot_nt(qz, k_new) + sel4[:, new_blk:new_blk + 1], NEG_INF)
        m = jnp.maximum(jnp.max(s, axis=-1, keepdims=True), jnp.max(s_new, axis=-1, keepdims=True))
        e = jnp.exp(s - m)
        e_new = jnp.exp(s_new - m)
        l = jnp.sum(e, axis=-1, keepdims=True) + jnp.sum(e_new, axis=-1, keepdims=True)
        o_s = (_dot_nt(e.astype(BF16), vt_past) + _dot(e_new.astype(BF16), v_new)) / jnp.maximum(l, TINY)

        s_w = jnp.where(wmask, _dot(qz, kwt), NEG_INF)
        s_wn = jnp.where(causal_new, _dot_nt(qz, kw_new), NEG_INF)
        m = jnp.maximum(jnp.max(s_w, axis=-1, keepdims=True), jnp.max(s_wn, axis=-1, keepdims=True))
        e = jnp.exp(s_w - m)
        e_new = jnp.exp(s_wn - m)
        l = jnp.sum(e, axis=-1, keepdims=True) + jnp.sum(e_new, axis=-1, keepdims=True)
        o_w = (_dot_nt(e.astype(BF16), vwt) + _dot(e_new.astype(BF16), vw_new)) / jnp.maximum(l, TINY)

        for g in range(GQ):
            h = kv * GQ + g
            rows = slice(g * R, (g + 1) * R)
            lanes = slice(kv * HEAD_DIM, (kv + 1) * HEAD_DIM)
            hl = slice(h * HEAD_DIM, (h + 1) * HEAD_DIM)
            out_ref[:, hl] = (oc_ref[:, hl] + gate[:, 3 * h + 1:3 * h + 2] * o_s[rows, lanes]
                              + gate[:, 3 * h + 2:3 * h + 3] * o_w[rows, lanes])


def _nsa_sample_attend(cache, win, pt_flat, q, gate, selneg, ocg, kvn, emat,
                       NB, tn, n_pages, page, base, win_base):
    past_len = n_pages * page
    wb = win.shape[2]
    rowblk = lambda w_: pl.BlockSpec((tn, w_), lambda n, pt: (n, 0))
    grid_spec = pltpu.PrefetchScalarGridSpec(
        num_scalar_prefetch=1,
        grid=(NB,),
        in_specs=_page_specs(n_pages, page, base, 1) + [
            rowblk(Q_W), rowblk(GATE_PAD), rowblk(2 * LANES), rowblk(Q_W), rowblk(KV_W),
            pl.BlockSpec((1, 2 * LANES, wb), lambda n, pt: (win_base + n, 0, 0)),
            pl.BlockSpec(emat.shape, lambda n, pt: (0, 0))],
        out_specs=rowblk(Q_W),
    )
    return pl.pallas_call(
        functools.partial(_nsa_sample_attend_kernel, n_pages, past_len=past_len, tn=tn),
        grid_spec=grid_spec,
        out_shape=jax.ShapeDtypeStruct((NB * tn, Q_W), F32),
        compiler_params=_params(("parallel",)),
        name="nsa_sample_attend",
    )(pt_flat, *([cache] * n_pages), q, gate, selneg, ocg, kvn, win, emat)


def _ffn(x1, gf_ref, w1_ref, w3_ref, w2_ref, hb_ref, acc_ref):
    hb_ref[...] = _rms(x1, gf_ref[...]).astype(BF16)
    acc_ref[...] = x1

    def body(c, carry):
        off = pl.multiple_of(c * FF_CHUNK, FF_CHUNK)
        h = hb_ref[...]
        a = _dot(h, w1_ref[:, pl.ds(off, FF_CHUNK)])
        b = _dot(h, w3_ref[:, pl.ds(off, FF_CHUNK)])
        gact = (a * _sigmoid(a) * b).astype(BF16)
        acc_ref[...] += _dot(gact, w2_ref[pl.ds(off, FF_CHUNK), :])
        return carry

    lax.fori_loop(0, D_FF // FF_CHUNK, body, 0)
    return acc_ref[...]


def _finish(x2, final, gfin_ref, out_ref):
    out_ref[...] = _rms(x2, gfin_ref[...]) if final else x2


def _even_tail_kernel(x_ref, attn_ref, u_ref, vn_ref, wmix_ref, bmix_ref, wo_ref,
                      gf_ref, w1_ref, w3_ref, w2_ref, gfin_ref, out_ref,
                      sgu_ref, hb_ref, acc_ref, *, chunk, final):
    tm = x_ref.shape[0]
    r = lax.broadcasted_iota(jnp.int32, (SGU_CHUNK, SGU_CHUNK), 0)
    c = lax.broadcasted_iota(jnp.int32, (SGU_CHUNK, SGU_CHUNK), 1)
    causal = (c <= r) & (r // chunk == c // chunk)
    for g in range(SGU_GROUPS):
        w = jnp.where(causal, wmix_ref[g], 0.0).astype(BF16)
        b = bmix_ref[g]
        lanes = slice(g * SGU_DIM, (g + 1) * SGU_DIM)
        for k in range(tm // SGU_CHUNK):
            rows = slice(k * SGU_CHUNK, (k + 1) * SGU_CHUNK)
            mix = _dot(w, vn_ref[rows, lanes].astype(BF16)) + b
            sgu_ref[rows, lanes] = (u_ref[rows, lanes] * mix).astype(BF16)
    x1 = (x_ref[...] + _dot(attn_ref[...].astype(BF16), wo_ref[0:Q_W, :])
          + _dot(sgu_ref[...], wo_ref[Q_W:Q_W + U_W, :]))
    _finish(_ffn(x1, gf_ref, w1_ref, w3_ref, w2_ref, hb_ref, acc_ref), final, gfin_ref, out_ref)


def _odd_tail_kernel(x_ref, z_ref, wa_ref, wb_ref, gf_ref, w1_ref, w3_ref, w2_ref, gfin_ref, out_ref,
                     hb_ref, acc_ref, *, final):
    z = z_ref[...]
    x1 = x_ref[...] + _dot(z, wa_ref[...]) * _sigmoid(_dot(z, wb_ref[...]))
    _finish(_ffn(x1, gf_ref, w1_ref, w3_ref, w2_ref, hb_ref, acc_ref), final, gfin_ref, out_ref)


def _ffn_specs():
    return [_full((1, D_MODEL)), _full((D_MODEL, D_FF)), _full((D_MODEL, D_FF)), _full((D_FF, D_MODEL)),
            _full((1, D_MODEL))]


def _even_tail(x, attn, u, vn, wmix, bmix, wo, gf, w1, w3, w2, gfin, chunk, final):
    T = x.shape[0]
    tm = min(TOKEN_TILE, T)
    row = lambda w_: pl.BlockSpec((tm, w_), lambda i: (i, 0))
    return pl.pallas_call(
        functools.partial(_even_tail_kernel, chunk=chunk, final=final),
        grid=(T // tm,),
        in_specs=[row(D_MODEL), row(Q_W), row(U_W), row(U_W), _full(wmix.shape), _full(bmix.shape),
                  _full(wo.shape)] + _ffn_specs(),
        out_specs=row(D_MODEL),
        out_shape=jax.ShapeDtypeStruct((T, D_MODEL), F32),
        scratch_shapes=[pltpu.VMEM((tm, U_W), BF16), pltpu.VMEM((tm, D_MODEL), BF16),
                        pltpu.VMEM((tm, D_MODEL), F32)],
        compiler_params=_params(("parallel",)),
        name="even_tail",
    )(x, attn, u, vn, wmix, bmix, wo, gf, w1, w3, w2, gfin)


def _odd_tail(x, z, wa, wb, gf, w1, w3, w2, gfin, final):
    T = x.shape[0]
    tm = min(TOKEN_TILE, T)
    row = lambda w_: pl.BlockSpec((tm, w_), lambda i: (i, 0))
    return pl.pallas_call(
        functools.partial(_odd_tail_kernel, final=final),
        grid=(T // tm,),
        in_specs=[row(D_MODEL), row(D_MODEL), _full(wa.shape), _full(wb.shape)] + _ffn_specs(),
        out_specs=row(D_MODEL),
        out_shape=jax.ShapeDtypeStruct((T, D_MODEL), F32),
        scratch_shapes=[pltpu.VMEM((tm, D_MODEL), BF16), pltpu.VMEM((tm, D_MODEL), F32)],
        compiler_params=_params(("parallel",)),
        name="odd_tail",
    )(x, z, wa, wb, gf, w1, w3, w2, gfin)


def _s5_disc_kernel(are_ref, aim_ref, ls_ref, bre_ref, bim_ref, abre_ref, abim_ref, bbre_ref, bbim_ref):
    a_re = are_ref[...]
    a_im = aim_ref[...]
    dt = jnp.exp(ls_ref[...])
    lr = a_re * dt
    li = a_im * dt
    mag = jnp.exp(lr)
    ab_re = mag * jnp.cos(li)
    ab_im = mag * jnp.sin(li)
    den = a_re * a_re + a_im * a_im
    nr = ab_re - 1.0
    cr = (nr * a_re + ab_im * a_im) / den
    cim = (ab_im * a_re - nr * a_im) / den
    b_re = bre_ref[...]
    b_im = bim_ref[...]
    abre_ref[...] = ab_re
    abim_ref[...] = ab_im
    bbre_ref[...] = cr * b_re - cim * b_im
    bbim_ref[...] = cr * b_im + cim * b_re


def _s5_disc(a_re, a_im, log_step, b_re, b_im):
    col = jax.ShapeDtypeStruct((S5_W, 1), F32)
    mat = jax.ShapeDtypeStruct((S5_W, S5_GROUP), F32)
    ls = jnp.broadcast_to(log_step[:, None], (S5_GROUPS, S5_STATE)).reshape(S5_W, 1)
    return pl.pallas_call(
        _s5_disc_kernel,
        out_shape=[col, col, mat, mat],
        name="s5_disc",
    )(a_re.reshape(S5_W, 1), a_im.reshape(S5_W, 1), ls,
      b_re.reshape(S5_W, S5_GROUP), b_im.reshape(S5_W, S5_GROUP))


S5_BLOCKS = 4
S5_BLK_CH = D_MODEL // S5_BLOCKS
S5_BLK_ST = S5_W // S5_BLOCKS


def _s5_input_states(ub, bw_ref, bu_ref):
    for k in range(S5_BLOCKS):
        r = _dot(ub[:, k * S5_BLK_CH:(k + 1) * S5_BLK_CH], bw_ref[k])
        bu_ref[:, k * S5_BLK_ST:(k + 1) * S5_BLK_ST] = r[:, 0:S5_BLK_ST]
        bu_ref[:, S5_W + k * S5_BLK_ST:S5_W + (k + 1) * S5_BLK_ST] = r[:, S5_BLK_ST:2 * S5_BLK_ST]


def _s5_output(u, h_ref, cre_ref, cim_ref, d_ref, store):
    for k in range(S5_BLOCKS):
        hr = h_ref[:, k * S5_BLK_ST:(k + 1) * S5_BLK_ST].astype(BF16)
        hi = h_ref[:, S5_W + k * S5_BLK_ST:S5_W + (k + 1) * S5_BLK_ST].astype(BF16)
        cols = slice(k * S5_BLK_CH, (k + 1) * S5_BLK_CH)
        y = _dot(hr, cre_ref[k]) - _dot(hi, cim_ref[k]) + d_ref[:, cols] * u[:, cols]
        store(cols, jax.nn.gelu(y).astype(BF16))


def _s5_prompt_kernel(x_ref, g_ref, ab_ref, bw_ref, cre_ref, cim_ref, d_ref,
                      z_ref, hre_ref, him_ref, bu_ref, st_ref):
    c = pl.program_id(1)
    tc = x_ref.shape[0]

    @pl.when(c == 0)
    def _():
        st_ref[...] = jnp.zeros_like(st_ref)

    u = _rms(x_ref[...], g_ref[...])
    _s5_input_states(u.astype(BF16), bw_ref, bu_ref)
    for k in range(S5_W // S5_LANE_CHUNK):
        lr = pl.ds(k * S5_LANE_CHUNK, S5_LANE_CHUNK)
        li = pl.ds(S5_W + k * S5_LANE_CHUNK, S5_LANE_CHUNK)
        ar = ab_ref[0:1, lr]
        ai = ab_ref[1:2, lr]

        def step(t, carry):
            hr, hi = carry
            row = pl.ds(t, 1)
            nr = ar * hr - ai * hi + bu_ref[row, lr]
            ni = ar * hi + ai * hr + bu_ref[row, li]
            bu_ref[row, lr] = nr
            bu_ref[row, li] = ni
            return nr, ni

        hr, hi = lax.fori_loop(0, tc, step, (st_ref[0:1, lr], st_ref[1:2, lr]), unroll=8)
        st_ref[0:1, lr] = hr
        st_ref[1:2, lr] = hi
    def store(cols, val):
        z_ref[:, cols] = val

    _s5_output(u, bu_ref, cre_ref, cim_ref, d_ref, store)
    hre_ref[0] = st_ref[0:1, :]
    him_ref[0] = st_ref[1:2, :]


def _s5_prompt(x, g, ab, bw, cre, cim, d, B, S):
    tc = min(S5_CHUNK_ROWS, S)
    nc = S // tc
    st = jax.ShapeDtypeStruct((B, 1, S5_W), F32)
    return pl.pallas_call(
        _s5_prompt_kernel,
        grid=(B, nc),
        in_specs=[pl.BlockSpec((tc, D_MODEL), lambda b, c: (b * nc + c, 0)), _full((1, D_MODEL)),
                  _full(ab.shape), _full(bw.shape), _full(cre.shape), _full(cim.shape), _full((1, D_MODEL))],
        out_specs=[pl.BlockSpec((tc, D_MODEL), lambda b, c: (b * nc + c, 0)),
                   pl.BlockSpec((1, 1, S5_W), lambda b, c: (b, 0, 0)),
                   pl.BlockSpec((1, 1, S5_W), lambda b, c: (b, 0, 0))],
        out_shape=[jax.ShapeDtypeStruct((B * S, D_MODEL), BF16), st, st],
        scratch_shapes=[pltpu.VMEM((tc, 2 * S5_W), F32), pltpu.VMEM((2, S5_W), F32)],
        compiler_params=_params(("parallel", "arbitrary")),
        name="s5_prompt",
    )(x, g, ab, bw, cre, cim, d)


def _s5_sample_kernel(x_ref, g_ref, ab_ref, bw_ref, cre_ref, cim_ref, d_ref, h0re_ref, h0im_ref,
                      z_ref, hre_ref, him_ref, bu_ref):
    tn, ns, _ = x_ref.shape
    u = _rms(x_ref[...].reshape(tn * ns, D_MODEL), g_ref[...])
    _s5_input_states(u.astype(BF16), bw_ref, bu_ref)
    ar = ab_ref[0:1, 0:S5_W]
    ai = ab_ref[1:2, 0:S5_W]
    hre_ref[...] = h0re_ref[...]
    him_ref[...] = h0im_ref[...]
    for j in range(tn):
        rows = slice(j * ns, (j + 1) * ns)
        hr = hre_ref[...]
        hi = him_ref[...]
        nr = ar * hr - ai * hi + bu_ref[rows, 0:S5_W]
        ni = ar * hi + ai * hr + bu_ref[rows, S5_W:2 * S5_W]
        hre_ref[...] = nr
        him_ref[...] = ni
        bu_ref[rows, 0:S5_W] = nr
        bu_ref[rows, S5_W:2 * S5_W] = ni

    def store(cols, val):
        z_ref[:, :, cols] = val.reshape(tn, ns, S5_BLK_CH)

    _s5_output(u, bu_ref, cre_ref, cim_ref, d_ref, store)


def _s5_sample(x_t, g, ab, bw, cre, cim, d, h0re, h0im):
    tn, NB, _ = x_t.shape
    ns = min(S5_SEQ_GROUP, NB)
    st = jax.ShapeDtypeStruct((NB, S5_W), F32)
    stb = pl.BlockSpec((ns, S5_W), lambda i: (i, 0))
    xb = pl.BlockSpec((tn, ns, D_MODEL), lambda i: (0, i, 0))
    return pl.pallas_call(
        _s5_sample_kernel,
        grid=(NB // ns,),
        in_specs=[xb, _full((1, D_MODEL)),
                  _full(ab.shape), _full(bw.shape), _full(cre.shape), _full(cim.shape), _full((1, D_MODEL)),
                  stb, stb],
        out_specs=[xb, stb, stb],
        out_shape=[jax.ShapeDtypeStruct((tn, NB, D_MODEL), BF16), st, st],
        scratch_shapes=[pltpu.VMEM((tn * ns, 2 * S5_W), F32)],
        compiler_params=_params(("parallel",)),
        name="s5_sample",
    )(x_t, g, ab, bw, cre, cim, d, h0re, h0im)


def _rope_tables(pos):
    half = ROPE_DIM // 2
    inv_freq = ROPE_THETA ** (-jnp.arange(half, dtype=F32) * (2.0 / ROPE_DIM))
    ang = pos.astype(F32)[:, None] * inv_freq[None, :]
    c, s = jnp.cos(ang), jnp.sin(ang)
    n = pos.shape[0]
    pad = HEAD_DIM - ROPE_DIM
    c64 = jnp.concatenate([c, c, jnp.ones((n, pad), F32)], axis=-1)
    s64 = jnp.concatenate([-s, s, jnp.zeros((n, pad), F32)], axis=-1)
    return jnp.tile(c64, (1, LANES // HEAD_DIM)), jnp.tile(s64, (1, LANES // HEAD_DIM))


def _w_in_layout(w):
    o = Q_W + KV_W
    gate = jnp.pad(w[:, o:o + GATE_W], ((0, 0), (0, GATE_PAD - GATE_W)))
    return jnp.concatenate([w[:, :o], w[:, o + GATE_W:], gate], axis=1).astype(BF16)


def _compress_layout(pe, w1, w2):
    eye = jnp.eye(2, dtype=F32)
    w1r = w1.reshape(2, 2, CMP_STRIDE, HEAD_DIM, HEAD_DIM)
    w1bd = jnp.einsum('cmsde,ca,kb->msckdabe', w1r, eye, eye)
    w1bd = w1bd.reshape(2 * CMP_STRIDE, 2 * LANES, 2 * LANES).astype(BF16)
    w2bd = jnp.einsum('cde,ca,kb->ckdabe', w2, eye, eye).reshape(2 * LANES, 2 * LANES).astype(BF16)
    per = pe.reshape(2, 2, CMP_STRIDE, HEAD_DIM)
    pel = jnp.broadcast_to(jnp.transpose(per, (1, 2, 0, 3))[:, :, :, None, :],
                           (2, CMP_STRIDE, 2, N_KV, HEAD_DIM)).reshape(2 * CMP_STRIDE, 2 * LANES)
    return pel, w1bd, w2bd


def _s5_layout(bb_re, bb_im, c_re, c_im):
    eye = jnp.eye(S5_GROUPS // S5_BLOCKS, dtype=F32)
    gl = S5_GROUPS // S5_BLOCKS
    bb = jnp.stack([bb_re, bb_im]).reshape(2, S5_BLOCKS, gl, S5_STATE, S5_GROUP)
    bw = jnp.einsum('rkgpc,gh->kgcrhp', bb, eye).reshape(S5_BLOCKS, S5_BLK_CH, 2 * S5_BLK_ST).astype(BF16)

    def cl(c):
        c = c.reshape(S5_BLOCKS, gl, S5_GROUP, S5_STATE)
        return jnp.einsum('kgcp,gh->kgphc', c, eye).reshape(S5_BLOCKS, S5_BLK_ST, S5_BLK_CH).astype(BF16)

    return bw, cl(c_re), cl(c_im)


def _block_onehot(n_keys):
    blk = jnp.arange(LANES)[:, None]
    key = jnp.arange(n_keys)[None, :]
    return (key // SLC_BLOCK == blk).astype(BF16)


def kernel(x_prompt, x_sample, cache_nsa_kv, cache_nsa_win, state_s5_re, state_s5_im, page_table, norm_mix, norm_ffn, norm_final, w_in, w_out, cmp_pe, cmp_w1, cmp_w2, sgu_ln_g, sgu_ln_b, sgu_w, sgu_b, s5_a_re, s5_a_im, s5_log_step, s5_b_re, s5_b_im, s5_c_re, s5_c_im, s5_d, glu_w_a, glu_w_b, ffn_w1, ffn_w3, ffn_w2):
    B, S, _ = x_prompt.shape
    NB, tn, _ = x_sample.shape
    n_even, pool, page = cache_nsa_kv.shape[:3]
    n_pages = page_table.shape[1]
    past_len = n_pages * page
    wb = cache_nsa_win.shape[2]
    assert S % KEY_TILE == 0 and S // SLC_BLOCK <= LANES and S >= WINDOW + Q_BLOCK
    assert past_len % SLC_BLOCK == 0 and tn <= CMP_STRIDE and SGU_CHUNK % tn == 0
    assert NB % NSA_SEQ_GROUP == 0 and -(-(past_len + tn) // SLC_BLOCK) <= LANES

    xp = x_prompt.reshape(B * S, D_MODEL)
    xs = x_sample.reshape(NB * tn, D_MODEL)
    cache = jnp.transpose(cache_nsa_kv, (0, 1, 3, 4, 5, 2)).reshape(n_even * pool, 4 * LANES, page)
    win = jnp.transpose(cache_nsa_win, (0, 1, 3, 4, 5, 2)).reshape(n_even * NB, 2 * LANES, wb)
    pt_flat = page_table.reshape(-1).astype(jnp.int32)
    cos_p, sin_p = _rope_tables(jnp.arange(S))
    cos_s, sin_s = _rope_tables(jnp.tile(past_len + jnp.arange(tn), NB))
    e_prompt = _block_onehot(S).T
    e_sample = _block_onehot(past_len)
    ncb_s = past_len // CMP_STRIDE
    tm_p = min(TOKEN_TILE, S)
    tm_s = min(TOKEN_TILE, NB * tn)
    row = lambda a: a.reshape(1, -1)

    kv_p, kv_s, win_p, win_s, sgu_v_s = [], [], [], [], []
    s5p_re, s5p_im, s5s_re, s5s_im = [], [], [], []
    for layer in range(DEPTH):
        final = layer == DEPTH - 1
        ffn = (row(norm_ffn[layer]), ffn_w1[layer].astype(BF16), ffn_w3[layer].astype(BF16),
               ffn_w2[layer].astype(BF16), row(norm_final))
        gmix = row(norm_mix[layer])
        if layer % 2 == 0:
            e = layer // 2
            wi = _w_in_layout(w_in[e])
            lng, lnb = row(sgu_ln_g[e]), row(sgu_ln_b[e])
            pel, w1bd, w2bd = _compress_layout(cmp_pe[e], cmp_w1[e], cmp_w2[e])
            qp, kvp, kvbp, gp, up, vnp = _inproj(xp, gmix, wi, cos_p, sin_p, lng, lnb, tm_p)
            qs, kvs, _, gs, us, vns = _inproj(xs, gmix, wi, cos_s, sin_s, lng, lnb, tm_s)
            ckp, cvp = _compress_prompt(kvp, B, S, pel, w1bd, w2bd)
            ap = _nsa_prompt(qp, gp, ckp, cvp, kvbp, e_prompt, B, S)
            cks, cvs = _compress_sample(cache, pt_flat, NB, n_pages, page, e * pool, pel, w1bd, w2bd)
            selneg, ocg = _nsa_sample_select(qs, gs, cks, cvs, NB, tn, past_len, ncb_s)
            a_s = _nsa_sample_attend(cache, win, pt_flat, qs, gs, selneg, ocg, kvs, e_sample,
                                     NB, tn, n_pages, page, e * pool, e * NB)
            wo = w_out[e].astype(BF16)
            bmix_p = sgu_b[e][:, :, None]
            reps = SGU_CHUNK // tn
            wmix_s = jnp.tile(sgu_w[e][:, :tn, :tn], (1, reps, reps))
            bmix_s = jnp.tile(sgu_b[e][:, :tn], (1, reps))[:, :, None]
            xp = _even_tail(xp, ap, up, vnp, sgu_w[e], bmix_p, wo, *ffn, chunk=SGU_CHUNK, final=final)
            xs = _even_tail(xs, a_s, us, vns, wmix_s, bmix_s, wo, *ffn, chunk=tn, final=final)
            kvp5 = kvp.reshape(B, S, 6, N_KV, HEAD_DIM)
            kvs5 = kvs.reshape(NB, tn, 6, N_KV, HEAD_DIM)
            kv_p.append(kvp5[:, :, 0:4])
            kv_s.append(kvs5[:, :, 0:4])
            win_p.append(kvp5[:, S - min(WINDOW, S):, 4:6])
            win_s.append(kvs5[:, :, 4:6])
            sgu_v_s.append(vns.reshape(NB, tn, SGU_GROUPS, SGU_DIM))
        else:
            o = layer // 2
            ab_re, ab_im, bb_re, bb_im = _s5_disc(s5_a_re[o], s5_a_im[o], s5_log_step[o], s5_b_re[o], s5_b_im[o])
            ab = jnp.concatenate([ab_re.reshape(1, S5_W), ab_im.reshape(1, S5_W)], axis=0)
            bw, cre, cim = _s5_layout(bb_re.reshape(S5_GROUPS, S5_STATE, S5_GROUP),
                                      bb_im.reshape(S5_GROUPS, S5_STATE, S5_GROUP), s5_c_re[o], s5_c_im[o])
            d = row(s5_d[o])
            zp, hpr, hpi = _s5_prompt(xp, gmix, ab, bw, cre, cim, d, B, S)
            xs_t = jnp.transpose(xs.reshape(NB, tn, D_MODEL), (1, 0, 2))
            zs_t, hsr, hsi = _s5_sample(xs_t, gmix, ab, bw, cre, cim, d,
                                        state_s5_re[o].reshape(NB, S5_W), state_s5_im[o].reshape(NB, S5_W))
            zs = jnp.transpose(zs_t, (1, 0, 2)).reshape(NB * tn, D_MODEL)
            wa, wb_ = glu_w_a[o].astype(BF16), glu_w_b[o].astype(BF16)
            xp = _odd_tail(xp, zp, wa, wb_, *ffn, final=final)
            xs = _odd_tail(xs, zs, wa, wb_, *ffn, final=final)
            s5p_re.append(hpr.reshape(B, S5_GROUPS, S5_STATE))
            s5p_im.append(hpi.reshape(B, S5_GROUPS, S5_STATE))
            s5s_re.append(hsr.reshape(NB, S5_GROUPS, S5_STATE))
            s5s_im.append(hsi.reshape(NB, S5_GROUPS, S5_STATE))
    return (xp.reshape(B, S, D_MODEL), xs.reshape(NB, tn, D_MODEL), jnp.stack(kv_p), jnp.stack(kv_s),
            jnp.stack(win_p), jnp.stack(win_s), jnp.stack(sgu_v_s), jnp.stack(s5p_re), jnp.stack(s5p_im),
            jnp.stack(s5s_re), jnp.stack(s5s_im))
```

```python
import functools
import math

import jax
import jax.numpy as jnp
from jax import lax
from jax.experimental import pallas as pl
from jax.experimental.pallas import tpu as pltpu

F32 = jnp.float32
BF16 = jnp.bfloat16

D_MODEL = 1024
DEPTH = 4
N_HEADS = 8
N_KV = 2
GQ = N_HEADS // N_KV
HEAD_DIM = 64
ROPE_DIM = 16
ROPE_THETA = 500000.0
CMP_BLOCK = 32
CMP_STRIDE = 16
SLC_BLOCK = 64
N_SELECT = 16
WINDOW = 512
Q_BLOCK = 128
SGU_GROUPS = 4
SGU_DIM = 128
SGU_CHUNK = 128
Q_W = N_HEADS * HEAD_DIM
KV_W = 6 * N_KV * HEAD_DIM
GATE_W = 3 * N_HEADS
U_W = SGU_GROUPS * SGU_DIM
S5_GROUP = 16
S5_GROUPS = D_MODEL // S5_GROUP
S5_STATE = 64
S5_W = S5_GROUPS * S5_STATE
D_FF = 2816
EPS = 1e-6
NEG_INF = -1e30
TINY = 1e-30
FORCE_SCORE = 1e4

LANES = 128
GATE_PAD = LANES
W_IN_COLS = Q_W + KV_W + 2 * U_W + GATE_PAD
FF_CHUNK = 256
TOKEN_TILE = 512
KEY_TILE = 512
S5_CHUNK_ROWS = 256
S5_SEQ_GROUP = 32
S5_SEGMENTS = 8
S5_TILE_GROUP = 8
S5_SEG_PAD = 4
NSA_SEQ_GROUP = 16
VMEM_LIMIT = 56 * 1024 * 1024


def _dot(a, b):
    return jnp.dot(a, b, preferred_element_type=F32)


def _dot_nt(a, b):
    return lax.dot_general(a, b, (((1,), (1,)), ((), ())), preferred_element_type=F32)


def _rms(x, g):
    return x * lax.rsqrt(jnp.mean(x * x, axis=-1, keepdims=True) + EPS) * g


def _sigmoid(x):
    return 1.0 / (1.0 + jnp.exp(-x))


def _params(sem):
    return pltpu.CompilerParams(dimension_semantics=sem, vmem_limit_bytes=VMEM_LIMIT)


def _full(shape):
    n = len(shape)
    return pl.BlockSpec(shape, lambda *_: (0,) * n)


def _inproj_kernel(x_ref, g_ref, w_ref, cos_ref, sin_ref, lng_ref, lnb_ref,
                   q_ref, kv_ref, kvb_ref, gate_ref, u_ref, vn_ref):
    x = x_ref[...]
    h = _rms(x, g_ref[...]).astype(BF16)
    cos = cos_ref[...]
    sin = sin_ref[...]
    lane = lax.broadcasted_iota(jnp.int32, cos.shape, 1)
    low = (lane % HEAD_DIM) < (ROPE_DIM // 2)

    def rope(t):
        rot = jnp.where(low, pltpu.roll(t, LANES - ROPE_DIM // 2, 1), pltpu.roll(t, ROPE_DIM // 2, 1))
        return t * cos + rot * sin

    pq = _dot(h, w_ref[:, 0:Q_W])
    for j in range(Q_W // LANES):
        q_ref[:, j * LANES:(j + 1) * LANES] = rope(pq[:, j * LANES:(j + 1) * LANES]) * (HEAD_DIM ** -0.5)
    pkv = _dot(h, w_ref[:, Q_W:Q_W + KV_W])
    for j in range(KV_W // LANES):
        t = pkv[:, j * LANES:(j + 1) * LANES]
        if j % 2 == 0:
            t = rope(t)
        kv_ref[:, j * LANES:(j + 1) * LANES] = t
        if j >= 2:
            kvb_ref[:, (j - 2) * LANES:(j - 1) * LANES] = t.astype(BF16)
    o = Q_W + KV_W
    u_ref[...] = jax.nn.gelu(_dot(h, w_ref[:, o:o + U_W]))
    v = jax.nn.gelu(_dot(h, w_ref[:, o + U_W:o + 2 * U_W]))
    vc = v - jnp.mean(v, axis=-1, keepdims=True)
    var = jnp.mean(vc * vc, axis=-1, keepdims=True)
    vn_ref[...] = vc * lax.rsqrt(var + EPS) * lng_ref[...] + lnb_ref[...]
    gate_ref[...] = _sigmoid(_dot(h, w_ref[:, o + 2 * U_W:o + 2 * U_W + GATE_PAD]))


def _inproj(x, g, w, cos, sin, lng, lnb, tm):
    T = x.shape[0]
    nt = T // tm
    ntab = cos.shape[0] // tm
    row = lambda w_: pl.BlockSpec((tm, w_), lambda i: (i, 0))
    tab = pl.BlockSpec((tm, LANES), lambda i: (i % ntab, 0))
    return pl.pallas_call(
        _inproj_kernel,
        grid=(nt,),
        in_specs=[row(D_MODEL), _full((1, D_MODEL)), _full((D_MODEL, W_IN_COLS)), tab, tab,
                  _full((1, U_W)), _full((1, U_W))],
        out_specs=[row(Q_W), row(KV_W), row(4 * LANES), row(GATE_PAD), row(U_W), row(U_W)],
        out_shape=[jax.ShapeDtypeStruct((T, Q_W), F32), jax.ShapeDtypeStruct((T, KV_W), F32),
                   jax.ShapeDtypeStruct((T, 4 * LANES), BF16), jax.ShapeDtypeStruct((T, GATE_PAD), F32),
                   jax.ShapeDtypeStruct((T, U_W), F32), jax.ShapeDtypeStruct((T, U_W), F32)],
        compiler_params=_params(("parallel",)),
        name="inproj",
    )(x, g, w, cos, sin, lng, lnb)


def _compress_core(load_rows, nch, pe_ref, w1_ref, w2_ref, acc0_ref, acc1_ref, k_ref, v_ref):
    for s in range(CMP_STRIDE):
        xs = load_rows(s)
        for m, acc in ((0, acc0_ref), (1, acc1_ref)):
            idx = m * CMP_STRIDE + s
            part = _dot((xs + pe_ref[idx:idx + 1, :]).astype(BF16), w1_ref[idx])
            if s == 0:
                acc[...] = part
            else:
                acc[...] += part
    hid = acc0_ref[...] + pltpu.roll(acc1_ref[...], nch - 1, 0)
    out = _dot(jax.nn.gelu(hid).astype(BF16), w2_ref[...])
    k_ref[...] = out[:, 0:LANES].astype(BF16).reshape(k_ref.shape)
    v_ref[...] = out[:, LANES:2 * LANES].astype(BF16).reshape(v_ref.shape)


def _compress_prompt_kernel(xk_ref, xv_ref, pe_ref, w1_ref, w2_ref, k_ref, v_ref, acc0_ref, acc1_ref):
    nch = xk_ref.shape[0] // CMP_STRIDE
    rows = lambda s: pl.ds(s, nch, stride=CMP_STRIDE)
    load = lambda s: jnp.concatenate([xk_ref[rows(s), :], xv_ref[rows(s), :]], axis=1)
    _compress_core(load, nch, pe_ref, w1_ref, w2_ref, acc0_ref, acc1_ref, k_ref, v_ref)


def _compress_prompt(kv, B, S, pe, w1, w2):
    nch = S // CMP_STRIDE
    out = jax.ShapeDtypeStruct((B, nch, LANES), BF16)
    return pl.pallas_call(
        _compress_prompt_kernel,
        grid=(B,),
        in_specs=[pl.BlockSpec((S, LANES), lambda b: (b, 0)), pl.BlockSpec((S, LANES), lambda b: (b, 1)),
                  _full(pe.shape), _full(w1.shape), _full(w2.shape)],
        out_specs=[pl.BlockSpec((1, nch, LANES), lambda b: (b, 0, 0))] * 2,
        out_shape=[out, out],
        scratch_shapes=[pltpu.VMEM((nch, 2 * LANES), F32)] * 2,
        compiler_params=_params(("parallel",)),
        name="compress_prompt",
    )(kv, kv, pe, w1, w2)


def _compress_sample_kernel(n_pages, pt_ref, *refs):
    pages = refs[:n_pages]
    pe_ref, w1_ref, w2_ref, k_ref, v_ref, acc0_ref, acc1_ref, xk_ref, xv_ref = refs[n_pages:]
    page = pages[0].shape[2]
    nch = n_pages * page // CMP_STRIDE
    for j, p in enumerate(pages):
        xt = p[0].T
        xk_ref[j * page:(j + 1) * page, :] = xt[:, 0:LANES]
        xv_ref[j * page:(j + 1) * page, :] = xt[:, LANES:2 * LANES]
    rows = lambda s: pl.ds(s, nch, stride=CMP_STRIDE)
    load = lambda s: jnp.concatenate([xk_ref[rows(s), :], xv_ref[rows(s), :]], axis=1)
    _compress_core(load, nch, pe_ref, w1_ref, w2_ref, acc0_ref, acc1_ref, k_ref, v_ref)


def _page_specs(n_pages, page, base, row_block):
    def spec(j):
        return pl.BlockSpec((1, 2 * LANES, page),
                            lambda n, pt: (base + pt[n * n_pages + j], row_block, 0))
    return [spec(j) for j in range(n_pages)]


def _compress_sample(cache_t, pt_flat, NB, n_pages, page, base, pe, w1, w2):
    nch = n_pages * page // CMP_STRIDE
    out = jax.ShapeDtypeStruct((NB * nch, LANES), BF16)
    cfull = lambda shape: pl.BlockSpec(shape, lambda n, pt: (0,) * len(shape))
    grid_spec = pltpu.PrefetchScalarGridSpec(
        num_scalar_prefetch=1,
        grid=(NB,),
        in_specs=_page_specs(n_pages, page, base, 0) + [cfull(pe.shape), cfull(w1.shape), cfull(w2.shape)],
        out_specs=[pl.BlockSpec((nch, LANES), lambda n, pt: (n, 0))] * 2,
        scratch_shapes=[pltpu.VMEM((nch, 2 * LANES), F32)] * 2 + [pltpu.VMEM((n_pages * page, LANES), F32)] * 2,
    )
    return pl.pallas_call(
        functools.partial(_compress_sample_kernel, n_pages),
        grid_spec=grid_spec,
        out_shape=[out, out],
        compiler_params=_params(("parallel",)),
        name="compress_sample",
    )(pt_flat, *([cache_t] * n_pages), pe, w1, w2)


def _group_queries(q, kv):
    lane = lax.broadcasted_iota(jnp.int32, (q.shape[0], LANES), 1)
    mine = (lane < HEAD_DIM) if kv == 0 else (lane >= HEAD_DIM)
    parts = []
    for g in range(GQ):
        h = kv * GQ + g
        t = q[:, (h // 2) * LANES:(h // 2 + 1) * LANES]
        if h % 2 != kv:
            t = pltpu.roll(t, HEAD_DIM, 1)
        parts.append(jnp.where(mine, t, 0.0))
    return jnp.concatenate(parts, axis=0).astype(BF16)


def _softmax_rows(s, mask):
    s = jnp.where(mask, s, NEG_INF)
    m = jnp.max(s, axis=-1, keepdims=True)
    e = jnp.where(mask, jnp.exp(s - m), 0.0)
    return e / jnp.maximum(jnp.sum(e, axis=-1, keepdims=True), TINY)


def _select_blocks(score):
    st = score.T
    blk = lax.broadcasted_iota(jnp.int32, st.shape, 0).astype(F32)

    def body(_, carry):
        sc, sel = carry
        m = jnp.max(sc, axis=0, keepdims=True)
        first = jnp.min(jnp.where(sc == m, blk, float(LANES)), axis=0, keepdims=True)
        hit = blk == first
        return jnp.where(hit, -2.0, sc), jnp.where(hit, 1.0, sel)

    _, sel = lax.fori_loop(0, N_SELECT, body, (st, jnp.zeros_like(st)))
    return sel.T


def _block_scores(imp, tq, n_slc):
    blk = lax.broadcasted_iota(jnp.int32, imp.shape, 1)
    cur = tq // SLC_BLOCK
    forced = (blk == 0) | (blk == cur) | (blk == cur - 1)
    allowed = blk * SLC_BLOCK <= tq
    score = jnp.where(forced, FORCE_SCORE, jnp.where(allowed, imp, -1.0))
    return jnp.where(blk < n_slc, score, -3.0)


def _cover(ci, sj):
    return ((ci * CMP_STRIDE < (sj + 1) * SLC_BLOCK) & (ci * CMP_STRIDE + CMP_BLOCK > sj * SLC_BLOCK))


def _nsa_prompt_kernel(q_ref, gate_ref, ck_ref, cv_ref, kvb_ref, et_ref, out_ref, *, seq):
    i = pl.program_id(1)
    R = Q_BLOCK
    s0 = i * R
    q = q_ref[...]
    gate = gate_ref[...]
    ncb = ck_ref.shape[1]
    n_cmp = seq // CMP_STRIDE - 1
    n_slc = seq // SLC_BLOCK
    tok4 = lax.broadcasted_iota(jnp.int32, (GQ * R, 1), 0) % R
    tq4 = s0 + tok4
    tq = s0 + lax.broadcasted_iota(jnp.int32, (R, 1), 0)
    ck = ck_ref[0]
    cv = cv_ref[0]
    ci = lax.broadcasted_iota(jnp.int32, (ncb, LANES), 0)
    sj = lax.broadcasted_iota(jnp.int32, (ncb, LANES), 1)
    cover = jnp.where(_cover(ci, sj) & (ci < n_cmp) & (sj < n_slc), 1.0, 0.0).astype(BF16)
    cmp_i = lax.broadcasted_iota(jnp.int32, (GQ * R, ncb), 1)
    cmp_mask = (cmp_i * CMP_STRIDE + (CMP_BLOCK - 1) <= tq4) & (cmp_i < n_cmp)
    n_tiles = (s0 + R + KEY_TILE - 1) // KEY_TILE
    wstart = pl.multiple_of(jnp.maximum(s0 - WINDOW, 0), R)
    wlen = WINDOW + R
    wpos = wstart + lax.broadcasted_iota(jnp.int32, (GQ * R, wlen), 1)
    wdiff = tq4 - wpos
    wmask = (wdiff >= 0) & (wdiff < WINDOW)
    kcol = lax.broadcasted_iota(jnp.int32, (GQ * R, KEY_TILE), 1)

    qzs, o_cs, scores = [], [], []
    for kv in range(N_KV):
        qz = _group_queries(q, kv)
        p_c = _softmax_rows(_dot_nt(qz, ck), cmp_mask).astype(BF16)
        o_cs.append(_dot(p_c, cv))
        imp4 = _dot(p_c, cover)
        imp = imp4[0:R] + imp4[R:2 * R] + imp4[2 * R:3 * R] + imp4[3 * R:4 * R]
        scores.append(_block_scores(imp, tq, n_slc))
        qzs.append(qz)
    sel = _select_blocks(jnp.concatenate(scores, axis=0))

    q_augs = []
    for kv in range(N_KV):
        selneg = jnp.where(sel[kv * R:(kv + 1) * R] > 0.5, 0.0, NEG_INF).astype(BF16)
        q_augs.append(jnp.concatenate([qzs[kv], jnp.concatenate([selneg] * GQ, axis=0)], axis=1))

    def tile_step(off, carry, diagonal):
        k_aug = jnp.concatenate([kvb_ref[pl.ds(off, KEY_TILE), 0:LANES],
                                 et_ref[pl.ds(off, KEY_TILE), :]], axis=1)
        vt = kvb_ref[pl.ds(off, KEY_TILE), LANES:2 * LANES]
        out = []
        for kv in range(N_KV):
            m, l, acc = carry[kv]
            s = _dot_nt(q_augs[kv], k_aug)
            if diagonal:
                s = jnp.where(off + kcol <= tq4, s, NEG_INF)
            m_new = jnp.maximum(m, jnp.max(s, axis=-1, keepdims=True))
            alpha = jnp.exp(m - m_new)
            p = jnp.exp(s - m_new)
            l = alpha * l + jnp.sum(p, axis=-1, keepdims=True)
            acc = alpha * acc + _dot(p.astype(BF16), vt)
            out.append((m_new, l, acc))
        return tuple(out)

    init1 = (jnp.full((GQ * R, 1), NEG_INF, F32), jnp.zeros((GQ * R, 1), F32),
             jnp.zeros((GQ * R, LANES), F32))
    carry = lax.fori_loop(
        0, n_tiles - 1, lambda t, c: tile_step(pl.multiple_of(t * KEY_TILE, KEY_TILE), c, False),
        (init1,) * N_KV)
    carry = tile_step(pl.multiple_of((n_tiles - 1) * KEY_TILE, KEY_TILE), carry, True)

    for kv in range(N_KV):
        qz, o_c = qzs[kv], o_cs[kv]
        _, l, acc = carry[kv]
        o_s = acc / jnp.maximum(l, TINY)

        kw = kvb_ref[pl.ds(wstart, wlen), 2 * LANES:3 * LANES]
        vw = kvb_ref[pl.ds(wstart, wlen), 3 * LANES:4 * LANES]
        p_w = _softmax_rows(_dot_nt(qz, kw), wmask).astype(BF16)
        o_w = _dot(p_w, vw)

        for g in range(GQ):
            h = kv * GQ + g
            rows = slice(g * R, (g + 1) * R)
            lanes = slice(kv * HEAD_DIM, (kv + 1) * HEAD_DIM)
            o = (gate[:, 3 * h:3 * h + 1] * o_c[rows, lanes]
                 + gate[:, 3 * h + 1:3 * h + 2] * o_s[rows, lanes]
                 + gate[:, 3 * h + 2:3 * h + 3] * o_w[rows, lanes])
            out_ref[:, h * HEAD_DIM:(h + 1) * HEAD_DIM] = o


def _nsa_prompt(q, gate, ck, cv, kvb, emat, B, S):
    nq = S // Q_BLOCK
    ncb = ck.shape[1]
    rowblk = lambda w_: pl.BlockSpec((Q_BLOCK, w_), lambda b, i: (b * nq + i, 0))
    return pl.pallas_call(
        functools.partial(_nsa_prompt_kernel, seq=S),
        grid=(B, nq),
        in_specs=[rowblk(Q_W), rowblk(GATE_PAD),
                  pl.BlockSpec((1, ncb, LANES), lambda b, i: (b, 0, 0)),
                  pl.BlockSpec((1, ncb, LANES), lambda b, i: (b, 0, 0)),
                  pl.BlockSpec((S, 4 * LANES), lambda b, i: (b, 0)),
                  pl.BlockSpec(emat.shape, lambda b, i: (0, 0))],
        out_specs=rowblk(Q_W),
        out_shape=jax.ShapeDtypeStruct((B * S, Q_W), F32),
        compiler_params=_params(("parallel", "arbitrary")),
        name="nsa_prompt",
    )(q, gate, ck, cv, kvb, emat)


def _nsa_sample_select_kernel(q_ref, gate_ref, ck_ref, cv_ref, sel_ref, oc_ref, *, past_len, tn, ncb):
    R = q_ref.shape[0]
    q = q_ref[...]
    gate = gate_ref[...]
    total = past_len + tn
    n_cmp = total // CMP_STRIDE - 1
    n_slc = -(-total // SLC_BLOCK)
    ncol = ck_ref.shape[0]
    row4 = lax.broadcasted_iota(jnp.int32, (GQ * R, 1), 0) % R
    tq4 = past_len + row4 % tn
    tq = past_len + lax.broadcasted_iota(jnp.int32, (R, 1), 0) % tn
    col = lax.broadcasted_iota(jnp.int32, (GQ * R, ncol), 1)
    ci = col % ncb
    cmp_mask = ((col // ncb == row4 // tn) & (ci * CMP_STRIDE + (CMP_BLOCK - 1) <= tq4) & (ci < n_cmp))
    cr = lax.broadcasted_iota(jnp.int32, (ncol, LANES), 0) % ncb
    sj = lax.broadcasted_iota(jnp.int32, (ncol, LANES), 1)
    cover = jnp.where(_cover(cr, sj) & (cr < n_cmp) & (sj < n_slc), 1.0, 0.0).astype(BF16)
    ck = ck_ref[...]
    cv = cv_ref[...]
    scores = []
    for kv in range(N_KV):
        qz = _group_queries(q, kv)
        p_c = _softmax_rows(_dot_nt(qz, ck), cmp_mask).astype(BF16)
        o_c = _dot(p_c, cv)
        imp4 = _dot(p_c, cover)
        imp = imp4[0:R] + imp4[R:2 * R] + imp4[2 * R:3 * R] + imp4[3 * R:4 * R]
        scores.append(_block_scores(imp, tq, n_slc))
        for g in range(GQ):
            h = kv * GQ + g
            oc_ref[:, h * HEAD_DIM:(h + 1) * HEAD_DIM] = (
                gate[:, 3 * h:3 * h + 1] * o_c[g * R:(g + 1) * R, kv * HEAD_DIM:(kv + 1) * HEAD_DIM])
    sel = _select_blocks(jnp.concatenate(scores, axis=0))
    for kv in range(N_KV):
        sel_ref[:, kv * LANES:(kv + 1) * LANES] = jnp.where(sel[kv * R:(kv + 1) * R] > 0.5, 0.0, NEG_INF)


def _nsa_sample_select(q, gate, ck, cv, NB, tn, past_len, ncb):
    R = NSA_SEQ_GROUP * tn
    rowblk = lambda w_: pl.BlockSpec((R, w_), lambda i: (i, 0))
    cblk = pl.BlockSpec((NSA_SEQ_GROUP * ncb, LANES), lambda i: (i, 0))
    return pl.pallas_call(
        functools.partial(_nsa_sample_select_kernel, past_len=past_len, tn=tn, ncb=ncb),
        grid=(NB // NSA_SEQ_GROUP,),
        in_specs=[rowblk(Q_W), rowblk(GATE_PAD), cblk, cblk],
        out_specs=[rowblk(2 * LANES), rowblk(Q_W)],
        out_shape=[jax.ShapeDtypeStruct((NB * tn, 2 * LANES), F32),
                   jax.ShapeDtypeStruct((NB * tn, Q_W), F32)],
        compiler_params=_params(("parallel",)),
        name="nsa_sample_select",
    )(q, gate, ck, cv)


def _nsa_sample_attend_kernel(n_pages, pt_ref, *refs, past_len, tn):
    pages = refs[:n_pages]
    q_ref, gate_ref, sel_ref, oc_ref, kvn_ref, win_ref, e_ref, out_ref = refs[n_pages:]
    R = tn
    q = q_ref[...]
    gate = gate_ref[...]
    kvn = kvn_ref[...]
    wb = win_ref.shape[2]
    new_blk = past_len // SLC_BLOCK
    tok4 = lax.broadcasted_iota(jnp.int32, (GQ * R, 1), 0) % R
    newcol = lax.broadcasted_iota(jnp.int32, (GQ * R, LANES), 1)
    causal_new = newcol <= tok4
    kt_past = jnp.concatenate([p[0, 0:LANES, :] for p in pages], axis=1).astype(BF16)
    vt_past = jnp.concatenate([p[0, LANES:2 * LANES, :] for p in pages], axis=1).astype(BF16)
    pad = jnp.zeros((LANES - tn, KV_W), F32)
    kvn = jnp.concatenate([kvn, pad], axis=0)
    k_new = kvn[:, 2 * LANES:3 * LANES].astype(BF16)
    v_new = kvn[:, 3 * LANES:4 * LANES].astype(BF16)
    kwt = win_ref[0, 0:LANES, :].astype(BF16)
    vwt = win_ref[0, LANES:2 * LANES, :].astype(BF16)
    kw_new = kvn[:, 4 * LANES:5 * LANES].astype(BF16)
    vw_new = kvn[:, 5 * LANES:6 * LANES].astype(BF16)
    wcol = lax.broadcasted_iota(jnp.int32, (GQ * R, wb), 1)
    wdiff = wb + tok4 - wcol
    wmask = (wdiff >= 0) & (wdiff < WINDOW) & (wcol >= wb - past_len)

    for kv in range(N_KV):
        qz = _group_queries(q, kv)
        sel4 = jnp.concatenate([sel_ref[:, kv * LANES:(kv + 1) * LANES]] * GQ, axis=0)
        s = _dot(qz, kt_past) + _dot(sel4.astype(BF16), e_ref[...])
        s_new = jnp.where(causal_new, _dot_nt(qz, k_new) + sel4[:, new_blk:new_blk + 1], NEG_INF)
        m = jnp.maximum(jnp.max(s, axis=-1, keepdims=True), jnp.max(s_new, axis=-1, keepdims=True))
        e = jnp.exp(s - m)
        e_new = jnp.exp(s_new - m)
        l = jnp.sum(e, axis=-1, keepdims=True) + jnp.sum(e_new, axis=-1, keepdims=True)
        o_s = (_dot_nt(e.astype(BF16), vt_past) + _dot(e_new.astype(BF16), v_new)) / jnp.maximum(l, TINY)

        s_w = jnp.where(wmask, _dot(qz, kwt), NEG_INF)
        s_wn = jnp.where(causal_new, _dot_nt(qz, kw_new), NEG_INF)
        m = jnp.maximum(jnp.max(s_w, axis=-1, keepdims=True), jnp.max(s_wn, axis=-1, keepdims=True))
        e = jnp.exp(s_w - m)
        e_new = jnp.exp(s_wn - m)
        l = jnp.sum(e, axis=-1, keepdims=True) + jnp.sum(e_new, axis=-1, keepdims=True)
        o_w = (_dot_nt(e.astype(BF16), vwt) + _dot(e_new.astype(BF16), vw_new)) / jnp.maximum(l, TINY)

        for g in range(GQ):
            h = kv * GQ + g
            rows = slice(g * R, (g + 1) * R)
            lanes = slice(kv * HEAD_DIM, (kv + 1) * HEAD_DIM)
            hl = slice(h * HEAD_DIM, (h + 1) * HEAD_DIM)
            out_ref[:, hl] = (oc_ref[:, hl] + gate[:, 3 * h + 1:3 * h + 2] * o_s[rows, lanes]
                              + gate[:, 3 * h + 2:3 * h + 3] * o_w[rows, lanes])


def _nsa_sample_attend(cache, win, pt_flat, q, gate, selneg, ocg, kvn, emat,
                       NB, tn, n_pages, page, base, win_base):
    past_len = n_pages * page
    wb = win.shape[2]
    rowblk = lambda w_: pl.BlockSpec((tn, w_), lambda n, pt: (n, 0))
    grid_spec = pltpu.PrefetchScalarGridSpec(
        num_scalar_prefetch=1,
        grid=(NB,),
        in_specs=_page_specs(n_pages, page, base, 1) + [
            rowblk(Q_W), rowblk(GATE_PAD), rowblk(2 * LANES), rowblk(Q_W), rowblk(KV_W),
            pl.BlockSpec((1, 2 * LANES, wb), lambda n, pt: (win_base + n, 0, 0)),
            pl.BlockSpec(emat.shape, lambda n, pt: (0, 0))],
        out_specs=rowblk(Q_W),
    )
    return pl.pallas_call(
        functools.partial(_nsa_sample_attend_kernel, n_pages, past_len=past_len, tn=tn),
        grid_spec=grid_spec,
        out_shape=jax.ShapeDtypeStruct((NB * tn, Q_W), F32),
        compiler_params=_params(("parallel",)),
        name="nsa_sample_attend",
    )(pt_flat, *([cache] * n_pages), q, gate, selneg, ocg, kvn, win, emat)


def _ffn(x1, gf_ref, w1_ref, w3_ref, w2_ref, hb_ref, acc_ref):
    hb_ref[...] = _rms(x1, gf_ref[...]).astype(BF16)
    acc_ref[...] = x1

    def body(c, carry):
        off = pl.multiple_of(c * FF_CHUNK, FF_CHUNK)
        h = hb_ref[...]
        a = _dot(h, w1_ref[:, pl.ds(off, FF_CHUNK)])
        b = _dot(h, w3_ref[:, pl.ds(off, FF_CHUNK)])
        gact = (a * _sigmoid(a) * b).astype(BF16)
        acc_ref[...] += _dot(gact, w2_ref[pl.ds(off, FF_CHUNK), :])
        return carry

    lax.fori_loop(0, D_FF // FF_CHUNK, body, 0)
    return acc_ref[...]


def _finish(x2, final, gfin_ref, out_ref):
    out_ref[...] = _rms(x2, gfin_ref[...]) if final else x2


def _even_tail_kernel(x_ref, attn_ref, u_ref, vn_ref, wmix_ref, bmix_ref, wo_ref,
                      gf_ref, w1_ref, w3_ref, w2_ref, gfin_ref, out_ref,
                      sgu_ref, hb_ref, acc_ref, *, chunk, final):
    tm = x_ref.shape[0]
    r = lax.broadcasted_iota(jnp.int32, (SGU_CHUNK, SGU_CHUNK), 0)
    c = lax.broadcasted_iota(jnp.int32, (SGU_CHUNK, SGU_CHUNK), 1)
    causal = (c <= r) & (r // chunk == c // chunk)
    for g in range(SGU_GROUPS):
        w = jnp.where(causal, wmix_ref[g], 0.0).astype(BF16)
        b = bmix_ref[g]
        lanes = slice(g * SGU_DIM, (g + 1) * SGU_DIM)
        for k in range(tm // SGU_CHUNK):
            rows = slice(k * SGU_CHUNK, (k + 1) * SGU_CHUNK)
            mix = _dot(w, vn_ref[rows, lanes].astype(BF16)) + b
            sgu_ref[rows, lanes] = (u_ref[rows, lanes] * mix).astype(BF16)
    x1 = (x_ref[...] + _dot(attn_ref[...].astype(BF16), wo_ref[0:Q_W, :])
          + _dot(sgu_ref[...], wo_ref[Q_W:Q_W + U_W, :]))
    _finish(_ffn(x1, gf_ref, w1_ref, w3_ref, w2_ref, hb_ref, acc_ref), final, gfin_ref, out_ref)


def _odd_tail_kernel(x_ref, z_ref, wa_ref, wb_ref, gf_ref, w1_ref, w3_ref, w2_ref, gfin_ref, out_ref,
                     hb_ref, acc_ref, *, final):
    z = z_ref[...]
    x1 = x_ref[...] + _dot(z, wa_ref[...]) * _sigmoid(_dot(z, wb_ref[...]))
    _finish(_ffn(x1, gf_ref, w1_ref, w3_ref, w2_ref, hb_ref, acc_ref), final, gfin_ref, out_ref)


def _ffn_specs():
    return [_full((1, D_MODEL)), _full((D_MODEL, D_FF)), _full((D_MODEL, D_FF)), _full((D_FF, D_MODEL)),
            _full((1, D_MODEL))]


def _even_tail(x, attn, u, vn, wmix, bmix, wo, gf, w1, w3, w2, gfin, chunk, final):
    T = x.shape[0]
    tm = min(TOKEN_TILE, T)
    row = lambda w_: pl.BlockSpec((tm, w_), lambda i: (i, 0))
    return pl.pallas_call(
        functools.partial(_even_tail_kernel, chunk=chunk, final=final),
        grid=(T // tm,),
        in_specs=[row(D_MODEL), row(Q_W), row(U_W), row(U_W), _full(wmix.shape), _full(bmix.shape),
                  _full(wo.shape)] + _ffn_specs(),
        out_specs=row(D_MODEL),
        out_shape=jax.ShapeDtypeStruct((T, D_MODEL), F32),
        scratch_shapes=[pltpu.VMEM((tm, U_W), BF16), pltpu.VMEM((tm, D_MODEL), BF16),
                        pltpu.VMEM((tm, D_MODEL), F32)],
        compiler_params=_params(("parallel",)),
        name="even_tail",
    )(x, attn, u, vn, wmix, bmix, wo, gf, w1, w3, w2, gfin)


def _odd_tail(x, z, wa, wb, gf, w1, w3, w2, gfin, final):
    T = x.shape[0]
    tm = min(TOKEN_TILE, T)
    row = lambda w_: pl.BlockSpec((tm, w_), lambda i: (i, 0))
    return pl.pallas_call(
        functools.partial(_odd_tail_kernel, final=final),
        grid=(T // tm,),
        in_specs=[row(D_MODEL), row(D_MODEL), _full(wa.shape), _full(wb.shape)] + _ffn_specs(),
        out_specs=row(D_MODEL),
        out_shape=jax.ShapeDtypeStruct((T, D_MODEL), F32),
        scratch_shapes=[pltpu.VMEM((tm, D_MODEL), BF16), pltpu.VMEM((tm, D_MODEL), F32)],
        compiler_params=_params(("parallel",)),
        name="odd_tail",
    )(x, z, wa, wb, gf, w1, w3, w2, gfin)


def _s5_disc_kernel(are_ref, aim_ref, ls_ref, bre_ref, bim_ref, abre_ref, abim_ref, bbre_ref, bbim_ref):
    a_re = are_ref[...]
    a_im = aim_ref[...]
    dt = jnp.exp(ls_ref[...])
    lr = a_re * dt
    li = a_im * dt
    mag = jnp.exp(lr)
    ab_re = mag * jnp.cos(li)
    ab_im = mag * jnp.sin(li)
    den = a_re * a_re + a_im * a_im
    nr = ab_re - 1.0
    cr = (nr * a_re + ab_im * a_im) / den
    cim = (ab_im * a_re - nr * a_im) / den
    b_re = bre_ref[...]
    b_im = bim_ref[...]
    abre_ref[...] = ab_re
    abim_ref[...] = ab_im
    bbre_ref[...] = cr * b_re - cim * b_im
    bbim_ref[...] = cr * b_im + cim * b_re


def _s5_disc(a_re, a_im, log_step, b_re, b_im):
    col = jax.ShapeDtypeStruct((S5_W, 1), F32)
    mat = jax.ShapeDtypeStruct((S5_W, S5_GROUP), F32)
    ls = jnp.broadcast_to(log_step[:, None], (S5_GROUPS, S5_STATE)).reshape(S5_W, 1)
    return pl.pallas_call(
        _s5_disc_kernel,
        out_shape=[col, col, mat, mat],
        name="s5_disc",
    )(a_re.reshape(S5_W, 1), a_im.reshape(S5_W, 1), ls,
      b_re.reshape(S5_W, S5_GROUP), b_im.reshape(S5_W, S5_GROUP))


S5_BLOCKS = 4
S5_BLK_CH = D_MODEL // S5_BLOCKS
S5_BLK_ST = S5_W // S5_BLOCKS


def _s5_input_states(ub, bw_ref, bu_ref):
    for k in range(S5_BLOCKS):
        r = _dot(ub[:, k * S5_BLK_CH:(k + 1) * S5_BLK_CH], bw_ref[k])
        bu_ref[:, k * S5_BLK_ST:(k + 1) * S5_BLK_ST] = r[:, 0:S5_BLK_ST]
        bu_ref[:, S5_W + k * S5_BLK_ST:S5_W + (k + 1) * S5_BLK_ST] = r[:, S5_BLK_ST:2 * S5_BLK_ST]


def _s5_output(u, h_ref, cre_ref, cim_ref, d_ref, store):
    for k in range(S5_BLOCKS):
        hr = h_ref[:, k * S5_BLK_ST:(k + 1) * S5_BLK_ST].astype(BF16)
        hi = h_ref[:, S5_W + k * S5_BLK_ST:S5_W + (k + 1) * S5_BLK_ST].astype(BF16)
        cols = slice(k * S5_BLK_CH, (k + 1) * S5_BLK_CH)
        y = _dot(hr, cre_ref[k]) - _dot(hi, cim_ref[k]) + d_ref[:, cols] * u[:, cols]
        store(cols, jax.nn.gelu(y).astype(BF16))


def _s5_prompt_kernel(x_ref, g_ref, ab_ref, bw_ref, cre_ref, cim_ref, d_ref,
                      z_ref, hre_ref, him_ref, bu_ref, st_ref, pw_ref, up_ref):
    c = pl.program_id(1)
    tc = x_ref.shape[0]
    seg = tc // S5_SEGMENTS
    ntile = S5_W // LANES

    @pl.when(c == 0)
    def _():
        st_ref[...] = jnp.zeros_like(st_ref)
        ar = ab_ref[0:1, :]
        ai = ab_ref[1:2, :]

        def put(t, pr, pi):
            for j in range(ntile):
                pw_ref[j, pl.ds(t, 1), :] = pr[:, j * LANES:(j + 1) * LANES]
                pw_ref[ntile + j, pl.ds(t, 1), :] = pi[:, j * LANES:(j + 1) * LANES]

        def power(t, carry):
            pr, pi = carry
            nr = ar * pr - ai * pi
            ni = ar * pi + ai * pr
            put(t, nr, ni)
            return nr, ni

        put(0, ar, ai)
        lax.fori_loop(1, seg, power, (ar, ai))

    pitch = seg + S5_SEG_PAD
    u = _rms(x_ref[...], g_ref[...])
    up_ref[...] = jnp.zeros_like(up_ref)
    for s in range(S5_SEGMENTS):
        up_ref[s * pitch:s * pitch + seg, :] = u[s * seg:(s + 1) * seg, :]
    up = up_ref[...]
    ub = up.astype(BF16)
    for k in range(S5_BLOCKS):
        r = _dot(ub[:, k * S5_BLK_CH:(k + 1) * S5_BLK_CH], bw_ref[k])
        for j in range(S5_BLK_ST // LANES):
            t = k * (S5_BLK_ST // LANES) + j
            bu_ref[t] = r[:, j * LANES:(j + 1) * LANES]
            bu_ref[ntile + t] = r[:, S5_BLK_ST + j * LANES:S5_BLK_ST + (j + 1) * LANES]

    for t0 in range(0, ntile, S5_TILE_GROUP):
        tiles = range(t0, t0 + S5_TILE_GROUP)
        rep = lambda v: jnp.broadcast_to(v, (S5_SEGMENTS, LANES))
        a = [(rep(ab_ref[0:1, t * LANES:(t + 1) * LANES]), rep(ab_ref[1:2, t * LANES:(t + 1) * LANES]))
             for t in tiles]

        def local(s, hc):
            rows = pl.ds(s, S5_SEGMENTS, stride=pitch)
            out = []
            for (ar, ai), (hr, hi), t in zip(a, hc, tiles):
                nr = ar * hr - ai * hi + bu_ref[t, rows, :]
                ni = ar * hi + ai * hr + bu_ref[ntile + t, rows, :]
                bu_ref[t, rows, :] = nr
                bu_ref[ntile + t, rows, :] = ni
                out.append((nr, ni))
            return tuple(out)

        zero = jnp.zeros((S5_SEGMENTS, LANES), F32)
        ends = lax.fori_loop(0, seg, local, ((zero, zero),) * S5_TILE_GROUP, unroll=4)
        h0 = []
        for (er, ei), t in zip(ends, tiles):
            lanes = slice(t * LANES, (t + 1) * LANES)
            qr = pw_ref[t, seg - 1:seg, :]
            qi = pw_ref[ntile + t, seg - 1:seg, :]
            hr = st_ref[0:1, lanes]
            hi = st_ref[1:2, lanes]
            hrs, his = [], []
            for s in range(S5_SEGMENTS):
                hrs.append(hr)
                his.append(hi)
                hr, hi = (er[s:s + 1] + qr * hr - qi * hi, ei[s:s + 1] + qr * hi + qi * hr)
            st_ref[0:1, lanes] = hr
            st_ref[1:2, lanes] = hi
            h0.append((jnp.concatenate(hrs, axis=0), jnp.concatenate(his, axis=0)))

        def fix(s, carry_):
            rows = pl.ds(s, S5_SEGMENTS, stride=pitch)
            for (h0r, h0i), t in zip(h0, tiles):
                pr = pw_ref[t, pl.ds(s, 1), :]
                pi = pw_ref[ntile + t, pl.ds(s, 1), :]
                bu_ref[t, rows, :] = bu_ref[t, rows, :] + (pr * h0r - pi * h0i)
                bu_ref[ntile + t, rows, :] = bu_ref[ntile + t, rows, :] + (pr * h0i + pi * h0r)
            return carry_

        lax.fori_loop(0, seg, fix, 0, unroll=2)

    per = S5_BLK_ST // LANES
    for k in range(S5_BLOCKS):
        hr = jnp.concatenate([bu_ref[k * per + j] for j in range(per)], axis=1).astype(BF16)
        hi = jnp.concatenate([bu_ref[ntile + k * per + j] for j in range(per)], axis=1).astype(BF16)
        cols = slice(k * S5_BLK_CH, (k + 1) * S5_BLK_CH)
        y = _dot(hr, cre_ref[k]) - _dot(hi, cim_ref[k]) + d_ref[:, cols] * up[:, cols]
        zp = jax.nn.gelu(y)
        for s in range(S5_SEGMENTS):
            z_ref[s * seg:(s + 1) * seg, cols] = zp[s * pitch:s * pitch + seg, :].astype(BF16)
    hre_ref[0] = st_ref[0:1, :]
    him_ref[0] = st_ref[1:2, :]


def _s5_prompt(x, g, ab, bw, cre, cim, d, B, S):
    tc = min(S5_CHUNK_ROWS, S)
    nc = S // tc
    padded = tc + S5_SEGMENTS * S5_SEG_PAD
    st = jax.ShapeDtypeStruct((B, 1, S5_W), F32)
    return pl.pallas_call(
        _s5_prompt_kernel,
        grid=(B, nc),
        in_specs=[pl.BlockSpec((tc, D_MODEL), lambda b, c: (b * nc + c, 0)), _full((1, D_MODEL)),
                  _full(ab.shape), _full(bw.shape), _full(cre.shape), _full(cim.shape), _full((1, D_MODEL))],
        out_specs=[pl.BlockSpec((tc, D_MODEL), lambda b, c: (b * nc + c, 0)),
                   pl.BlockSpec((1, 1, S5_W), lambda b, c: (b, 0, 0)),
                   pl.BlockSpec((1, 1, S5_W), lambda b, c: (b, 0, 0))],
        out_shape=[jax.ShapeDtypeStruct((B * S, D_MODEL), BF16), st, st],
        scratch_shapes=[pltpu.VMEM((2 * S5_W // LANES, padded, LANES), F32), pltpu.VMEM((2, S5_W), F32),
                        pltpu.VMEM((2 * S5_W // LANES, tc // S5_SEGMENTS, LANES), F32),
                        pltpu.VMEM((padded, D_MODEL), F32)],
        compiler_params=_params(("parallel", "arbitrary")),
        name="s5_prompt",
    )(x, g, ab, bw, cre, cim, d)


def _s5_sample_kernel(x_ref, g_ref, ab_ref, bw_ref, cre_ref, cim_ref, d_ref, h0re_ref, h0im_ref,
                      z_ref, hre_ref, him_ref, bu_ref):
    tn, ns, _ = x_ref.shape
    u = _rms(x_ref[...].reshape(tn * ns, D_MODEL), g_ref[...])
    _s5_input_states(u.astype(BF16), bw_ref, bu_ref)
    ar = ab_ref[0:1, 0:S5_W]
    ai = ab_ref[1:2, 0:S5_W]
    hre_ref[...] = h0re_ref[...]
    him_ref[...] = h0im_ref[...]
    for j in range(tn):
        rows = slice(j * ns, (j + 1) * ns)
        hr = hre_ref[...]
        hi = him_ref[...]
        nr = ar * hr - ai * hi + bu_ref[rows, 0:S5_W]
        ni = ar * hi + ai * hr + bu_ref[rows, S5_W:2 * S5_W]
        hre_ref[...] = nr
        him_ref[...] = ni
        bu_ref[rows, 0:S5_W] = nr
        bu_ref[rows, S5_W:2 * S5_W] = ni

    def store(cols, val):
        z_ref[:, :, cols] = val.reshape(tn, ns, S5_BLK_CH)

    _s5_output(u, bu_ref, cre_ref, cim_ref, d_ref, store)


def _s5_sample(x_t, g, ab, bw, cre, cim, d, h0re, h0im):
    tn, NB, _ = x_t.shape
    ns = min(S5_SEQ_GROUP, NB)
    st = jax.ShapeDtypeStruct((NB, S5_W), F32)
    stb = pl.BlockSpec((ns, S5_W), lambda i: (i, 0))
    xb = pl.BlockSpec((tn, ns, D_MODEL), lambda i: (0, i, 0))
    return pl.pallas_call(
        _s5_sample_kernel,
        grid=(NB // ns,),
        in_specs=[xb, _full((1, D_MODEL)),
                  _full(ab.shape), _full(bw.shape), _full(cre.shape), _full(cim.shape), _full((1, D_MODEL)),
                  stb, stb],
        out_specs=[xb, stb, stb],
        out_shape=[jax.ShapeDtypeStruct((tn, NB, D_MODEL), BF16), st, st],
        scratch_shapes=[pltpu.VMEM((tn * ns, 2 * S5_W), F32)],
        compiler_params=_params(("parallel",)),
        name="s5_sample",
    )(x_t, g, ab, bw, cre, cim, d, h0re, h0im)


def _rope_tables(pos):
    half = ROPE_DIM // 2
    inv_freq = ROPE_THETA ** (-jnp.arange(half, dtype=F32) * (2.0 / ROPE_DIM))
    ang = pos.astype(F32)[:, None] * inv_freq[None, :]
    c, s = jnp.cos(ang), jnp.sin(ang)
    n = pos.shape[0]
    pad = HEAD_DIM - ROPE_DIM
    c64 = jnp.concatenate([c, c, jnp.ones((n, pad), F32)], axis=-1)
    s64 = jnp.concatenate([-s, s, jnp.zeros((n, pad), F32)], axis=-1)
    return jnp.tile(c64, (1, LANES // HEAD_DIM)), jnp.tile(s64, (1, LANES // HEAD_DIM))


def _w_in_layout(w):
    o = Q_W + KV_W
    gate = jnp.pad(w[:, o:o + GATE_W], ((0, 0), (0, GATE_PAD - GATE_W)))
    return jnp.concatenate([w[:, :o], w[:, o + GATE_W:], gate], axis=1).astype(BF16)


def _compress_layout(pe, w1, w2):
    w1r = w1.reshape(2, 2 * CMP_STRIDE, HEAD_DIM, HEAD_DIM)
    w1bd = jnp.zeros((2 * CMP_STRIDE, 2 * LANES, 2 * LANES), BF16)
    w2bd = jnp.zeros((2 * LANES, 2 * LANES), BF16)
    for c in range(2):
        for k in range(N_KV):
            o = (c * N_KV + k) * HEAD_DIM
            w1bd = w1bd.at[:, o:o + HEAD_DIM, o:o + HEAD_DIM].set(w1r[c].astype(BF16))
            w2bd = w2bd.at[o:o + HEAD_DIM, o:o + HEAD_DIM].set(w2[c].astype(BF16))
    per = pe.reshape(2, 2, CMP_STRIDE, HEAD_DIM)
    pel = jnp.broadcast_to(jnp.transpose(per, (1, 2, 0, 3))[:, :, :, None, :],
                           (2, CMP_STRIDE, 2, N_KV, HEAD_DIM)).reshape(2 * CMP_STRIDE, 2 * LANES)
    return pel, w1bd, w2bd


def _s5_layout(bb_re, bb_im, c_re, c_im):
    eye = jnp.eye(S5_GROUPS // S5_BLOCKS, dtype=F32)
    gl = S5_GROUPS // S5_BLOCKS
    bb = jnp.stack([bb_re, bb_im]).reshape(2, S5_BLOCKS, gl, S5_STATE, S5_GROUP)
    bw = jnp.einsum('rkgpc,gh->kgcrhp', bb, eye).reshape(S5_BLOCKS, S5_BLK_CH, 2 * S5_BLK_ST).astype(BF16)

    def cl(c):
        c = c.reshape(S5_BLOCKS, gl, S5_GROUP, S5_STATE)
        return jnp.einsum('kgcp,gh->kgphc', c, eye).reshape(S5_BLOCKS, S5_BLK_ST, S5_BLK_CH).astype(BF16)

    return bw, cl(c_re), cl(c_im)


def _block_onehot(n_keys):
    blk = jnp.arange(LANES)[:, None]
    key = jnp.arange(n_keys)[None, :]
    return (key // SLC_BLOCK == blk).astype(BF16)


def kernel(x_prompt, x_sample, cache_nsa_kv, cache_nsa_win, state_s5_re, state_s5_im, page_table, norm_mix, norm_ffn, norm_final, w_in, w_out, cmp_pe, cmp_w1, cmp_w2, sgu_ln_g, sgu_ln_b, sgu_w, sgu_b, s5_a_re, s5_a_im, s5_log_step, s5_b_re, s5_b_im, s5_c_re, s5_c_im, s5_d, glu_w_a, glu_w_b, ffn_w1, ffn_w3, ffn_w2):
    B, S, _ = x_prompt.shape
    NB, tn, _ = x_sample.shape
    n_even, pool, page = cache_nsa_kv.shape[:3]
    n_pages = page_table.shape[1]
    past_len = n_pages * page
    wb = cache_nsa_win.shape[2]
    assert S % KEY_TILE == 0 and S // SLC_BLOCK <= LANES and S >= WINDOW + Q_BLOCK
    assert past_len % SLC_BLOCK == 0 and tn <= CMP_STRIDE and SGU_CHUNK % tn == 0
    assert NB % NSA_SEQ_GROUP == 0 and -(-(past_len + tn) // SLC_BLOCK) <= LANES

    xp = x_prompt.reshape(B * S, D_MODEL)
    xs = x_sample.reshape(NB * tn, D_MODEL)
    cache = jnp.transpose(cache_nsa_kv, (0, 1, 3, 4, 5, 2)).reshape(n_even * pool, 4 * LANES, page)
    win = jnp.transpose(cache_nsa_win, (0, 1, 3, 4, 5, 2)).reshape(n_even * NB, 2 * LANES, wb)
    pt_flat = page_table.reshape(-1).astype(jnp.int32)
    cos_p, sin_p = _rope_tables(jnp.arange(S))
    cos_s, sin_s = _rope_tables(jnp.tile(past_len + jnp.arange(tn), NB))
    e_prompt = _block_onehot(S).T
    e_sample = _block_onehot(past_len)
    ncb_s = past_len // CMP_STRIDE
    tm_p = min(TOKEN_TILE, S)
    tm_s = min(TOKEN_TILE, NB * tn)
    row = lambda a: a.reshape(1, -1)

    kv_p, kv_s, win_p, win_s, sgu_v_s = [], [], [], [], []
    s5p_re, s5p_im, s5s_re, s5s_im = [], [], [], []
    for layer in range(DEPTH):
        final = layer == DEPTH - 1
        ffn = (row(norm_ffn[layer]), ffn_w1[layer].astype(BF16), ffn_w3[layer].astype(BF16),
               ffn_w2[layer].astype(BF16), row(norm_final))
        gmix = row(norm_mix[layer])
        if layer % 2 == 0:
            e = layer // 2
            wi = _w_in_layout(w_in[e])
            lng, lnb = row(sgu_ln_g[e]), row(sgu_ln_b[e])
            pel, w1bd, w2bd = _compress_layout(cmp_pe[e], cmp_w1[e], cmp_w2[e])
            qp, kvp, kvbp, gp, up, vnp = _inproj(xp, gmix, wi, cos_p, sin_p, lng, lnb, tm_p)
            qs, kvs, _, gs, us, vns = _inproj(xs, gmix, wi, cos_s, sin_s, lng, lnb, tm_s)
            ckp, cvp = _compress_prompt(kvp, B, S, pel, w1bd, w2bd)
            ap = _nsa_prompt(qp, gp, ckp, cvp, kvbp, e_prompt, B, S)
            cks, cvs = _compress_sample(cache, pt_flat, NB, n_pages, page, e * pool, pel, w1bd, w2bd)
            selneg, ocg = _nsa_sample_select(qs, gs, cks, cvs, NB, tn, past_len, ncb_s)
            a_s = _nsa_sample_attend(cache, win, pt_flat, qs, gs, selneg, ocg, kvs, e_sample,
                                     NB, tn, n_pages, page, e * pool, e * NB)
            wo = w_out[e].astype(BF16)
            bmix_p = sgu_b[e][:, :, None]
            reps = SGU_CHUNK // tn
            wmix_s = jnp.tile(sgu_w[e][:, :tn, :tn], (1, reps, reps))
            bmix_s = jnp.tile(sgu_b[e][:, :tn], (1, reps))[:, :, None]
            xp = _even_tail(xp, ap, up, vnp, sgu_w[e], bmix_p, wo, *ffn, chunk=SGU_CHUNK, final=final)
            xs = _even_tail(xs, a_s, us, vns, wmix_s, bmix_s, wo, *ffn, chunk=tn, final=final)
            kvp5 = kvp.reshape(B, S, 6, N_KV, HEAD_DIM)
            kvs5 = kvs.reshape(NB, tn, 6, N_KV, HEAD_DIM)
            kv_p.append(kvp5[:, :, 0:4])
            kv_s.append(kvs5[:, :, 0:4])
            win_p.append(kvp5[:, S - min(WINDOW, S):, 4:6])
            win_s.append(kvs5[:, :, 4:6])
            sgu_v_s.append(vns.reshape(NB, tn, SGU_GROUPS, SGU_DIM))
        else:
            o = layer // 2
            ab_re, ab_im, bb_re, bb_im = _s5_disc(s5_a_re[o], s5_a_im[o], s5_log_step[o], s5_b_re[o], s5_b_im[o])
            ab = jnp.concatenate([ab_re.reshape(1, S5_W), ab_im.reshape(1, S5_W)], axis=0)
            bw, cre, cim = _s5_layout(bb_re.reshape(S5_GROUPS, S5_STATE, S5_GROUP),
                                      bb_im.reshape(S5_GROUPS, S5_STATE, S5_GROUP), s5_c_re[o], s5_c_im[o])
            d = row(s5_d[o])
            zp, hpr, hpi = _s5_prompt(xp, gmix, ab, bw, cre, cim, d, B, S)
            xs_t = jnp.transpose(xs.reshape(NB, tn, D_MODEL), (1, 0, 2))
            zs_t, hsr, hsi = _s5_sample(xs_t, gmix, ab, bw, cre, cim, d,
                                        state_s5_re[o].reshape(NB, S5_W), state_s5_im[o].reshape(NB, S5_W))
            zs = jnp.transpose(zs_t, (1, 0, 2)).reshape(NB * tn, D_MODEL)
            wa, wb_ = glu_w_a[o].astype(BF16), glu_w_b[o].astype(BF16)
            xp = _odd_tail(xp, zp, wa, wb_, *ffn, final=final)
            xs = _odd_tail(xs, zs, wa, wb_, *ffn, final=final)
            s5p_re.append(hpr.reshape(B, S5_GROUPS, S5_STATE))
            s5p_im.append(hpi.reshape(B, S5_GROUPS, S5_STATE))
            s5s_re.append(hsr.reshape(NB, S5_GROUPS, S5_STATE))
            s5s_im.append(hsi.reshape(NB, S5_GROUPS, S5_STATE))
    return (xp.reshape(B, S, D_MODEL), xs.reshape(NB, tn, D_MODEL), jnp.stack(kv_p), jnp.stack(kv_s),
            jnp.stack(win_p), jnp.stack(win_s), jnp.stack(sgu_v_s), jnp.stack(s5p_re), jnp.stack(s5p_im),
            jnp.stack(s5s_re), jnp.stack(s5s_im))
```

```python
import functools
import math

import jax
import jax.numpy as jnp
from jax import lax
from jax.experimental import pallas as pl
from jax.experimental.pallas import tpu as pltpu

F32 = jnp.float32
BF16 = jnp.bfloat16

D_MODEL = 1024
DEPTH = 4
N_HEADS = 8
N_KV = 2
GQ = N_HEADS // N_KV
HEAD_DIM = 64
ROPE_DIM = 16
ROPE_THETA = 500000.0
CMP_BLOCK = 32
CMP_STRIDE = 16
SLC_BLOCK = 64
N_SELECT = 16
WINDOW = 512
Q_BLOCK = 128
SGU_GROUPS = 4
SGU_DIM = 128
SGU_CHUNK = 128
Q_W = N_HEADS * HEAD_DIM
KV_W = 6 * N_KV * HEAD_DIM
GATE_W = 3 * N_HEADS
U_W = SGU_GROUPS * SGU_DIM
S5_GROUP = 16
S5_GROUPS = D_MODEL // S5_GROUP
S5_STATE = 64
S5_W = S5_GROUPS * S5_STATE
D_FF = 2816
EPS = 1e-6
NEG_INF = -1e30
TINY = 1e-30
FORCE_SCORE = 1e4

LANES = 128
GATE_PAD = LANES
W_IN_COLS = Q_W + KV_W + 2 * U_W + GATE_PAD
FF_CHUNK = 256
TOKEN_TILE = 512
KEY_TILE = 512
SEL_SPLIT = 2
M_FLOOR = -1e29
S5_CHUNK_ROWS = 256
S5_SEQ_GROUP = 32
S5_SEGMENTS = 8
S5_TILE_GROUP = 8
S5_SEG_PAD = 4
NSA_SEQ_GROUP = 16
VMEM_LIMIT = 56 * 1024 * 1024


def _dot(a, b):
    return jnp.dot(a, b, preferred_element_type=F32)


def _dot_nt(a, b):
    return lax.dot_general(a, b, (((1,), (1,)), ((), ())), preferred_element_type=F32)


def _rms(x, g):
    return x * lax.rsqrt(jnp.mean(x * x, axis=-1, keepdims=True) + EPS) * g


def _sigmoid(x):
    return 1.0 / (1.0 + jnp.exp(-x))


def _params(sem):
    return pltpu.CompilerParams(dimension_semantics=sem, vmem_limit_bytes=VMEM_LIMIT)


def _full(shape):
    n = len(shape)
    return pl.BlockSpec(shape, lambda *_: (0,) * n)


def _inproj_kernel(x_ref, g_ref, w_ref, cos_ref, sin_ref, lng_ref, lnb_ref,
                   q_ref, kv_ref, kvb_ref, gate_ref, u_ref, vn_ref):
    x = x_ref[...]
    h = _rms(x, g_ref[...]).astype(BF16)
    cos = cos_ref[...]
    sin = sin_ref[...]
    lane = lax.broadcasted_iota(jnp.int32, cos.shape, 1)
    low = (lane % HEAD_DIM) < (ROPE_DIM // 2)

    def rope(t):
        rot = jnp.where(low, pltpu.roll(t, LANES - ROPE_DIM // 2, 1), pltpu.roll(t, ROPE_DIM // 2, 1))
        return t * cos + rot * sin

    pq = _dot(h, w_ref[:, 0:Q_W])
    for j in range(Q_W // LANES):
        q_ref[:, j * LANES:(j + 1) * LANES] = rope(pq[:, j * LANES:(j + 1) * LANES]) * (HEAD_DIM ** -0.5)
    pkv = _dot(h, w_ref[:, Q_W:Q_W + KV_W])
    for j in range(KV_W // LANES):
        t = pkv[:, j * LANES:(j + 1) * LANES]
        if j % 2 == 0:
            t = rope(t)
        kv_ref[:, j * LANES:(j + 1) * LANES] = t
        if j >= 2:
            kvb_ref[:, (j - 2) * LANES:(j - 1) * LANES] = t.astype(BF16)
    o = Q_W + KV_W
    u_ref[...] = jax.nn.gelu(_dot(h, w_ref[:, o:o + U_W]))
    v = jax.nn.gelu(_dot(h, w_ref[:, o + U_W:o + 2 * U_W]))
    vc = v - jnp.mean(v, axis=-1, keepdims=True)
    var = jnp.mean(vc * vc, axis=-1, keepdims=True)
    vn_ref[...] = vc * lax.rsqrt(var + EPS) * lng_ref[...] + lnb_ref[...]
    gate_ref[...] = _sigmoid(_dot(h, w_ref[:, o + 2 * U_W:o + 2 * U_W + GATE_PAD]))


def _inproj(x, g, w, cos, sin, lng, lnb, tm):
    T = x.shape[0]
    nt = T // tm
    ntab = cos.shape[0] // tm
    row = lambda w_: pl.BlockSpec((tm, w_), lambda i: (i, 0))
    tab = pl.BlockSpec((tm, LANES), lambda i: (i % ntab, 0))
    return pl.pallas_call(
        _inproj_kernel,
        grid=(nt,),
        in_specs=[row(D_MODEL), _full((1, D_MODEL)), _full((D_MODEL, W_IN_COLS)), tab, tab,
                  _full((1, U_W)), _full((1, U_W))],
        out_specs=[row(Q_W), row(KV_W), row(4 * LANES), row(GATE_PAD), row(U_W), row(U_W)],
        out_shape=[jax.ShapeDtypeStruct((T, Q_W), F32), jax.ShapeDtypeStruct((T, KV_W), F32),
                   jax.ShapeDtypeStruct((T, 4 * LANES), BF16), jax.ShapeDtypeStruct((T, GATE_PAD), F32),
                   jax.ShapeDtypeStruct((T, U_W), F32), jax.ShapeDtypeStruct((T, U_W), F32)],
        compiler_params=_params(("parallel",)),
        name="inproj",
    )(x, g, w, cos, sin, lng, lnb)


def _compress_core(load_rows, nch, pe_ref, w1_ref, w2_ref, acc0_ref, acc1_ref, k_ref, v_ref):
    for s in range(CMP_STRIDE):
        xs = load_rows(s)
        for m, acc in ((0, acc0_ref), (1, acc1_ref)):
            idx = m * CMP_STRIDE + s
            part = _dot((xs + pe_ref[idx:idx + 1, :]).astype(BF16), w1_ref[idx])
            if s == 0:
                acc[...] = part
            else:
                acc[...] += part
    hid = acc0_ref[...] + pltpu.roll(acc1_ref[...], nch - 1, 0)
    out = _dot(jax.nn.gelu(hid).astype(BF16), w2_ref[...])
    k_ref[...] = out[:, 0:LANES].astype(BF16).reshape(k_ref.shape)
    v_ref[...] = out[:, LANES:2 * LANES].astype(BF16).reshape(v_ref.shape)


def _compress_prompt_kernel(xk_ref, xv_ref, pe_ref, w1_ref, w2_ref, k_ref, v_ref, acc0_ref, acc1_ref):
    nch = xk_ref.shape[0] // CMP_STRIDE
    rows = lambda s: pl.ds(s, nch, stride=CMP_STRIDE)
    load = lambda s: jnp.concatenate([xk_ref[rows(s), :], xv_ref[rows(s), :]], axis=1)
    _compress_core(load, nch, pe_ref, w1_ref, w2_ref, acc0_ref, acc1_ref, k_ref, v_ref)


def _compress_prompt(kv, B, S, pe, w1, w2):
    nch = S // CMP_STRIDE
    out = jax.ShapeDtypeStruct((B, nch, LANES), BF16)
    return pl.pallas_call(
        _compress_prompt_kernel,
        grid=(B,),
        in_specs=[pl.BlockSpec((S, LANES), lambda b: (b, 0)), pl.BlockSpec((S, LANES), lambda b: (b, 1)),
                  _full(pe.shape), _full(w1.shape), _full(w2.shape)],
        out_specs=[pl.BlockSpec((1, nch, LANES), lambda b: (b, 0, 0))] * 2,
        out_shape=[out, out],
        scratch_shapes=[pltpu.VMEM((nch, 2 * LANES), F32)] * 2,
        compiler_params=_params(("parallel",)),
        name="compress_prompt",
    )(kv, kv, pe, w1, w2)


def _compress_sample_kernel(n_pages, pt_ref, *refs):
    pages = refs[:n_pages]
    pe_ref, w1_ref, w2_ref, k_ref, v_ref, acc0_ref, acc1_ref, xk_ref, xv_ref = refs[n_pages:]
    page = pages[0].shape[2]
    nch = n_pages * page // CMP_STRIDE
    for j, p in enumerate(pages):
        xt = p[0].T
        xk_ref[j * page:(j + 1) * page, :] = xt[:, 0:LANES]
        xv_ref[j * page:(j + 1) * page, :] = xt[:, LANES:2 * LANES]
    rows = lambda s: pl.ds(s, nch, stride=CMP_STRIDE)
    load = lambda s: jnp.concatenate([xk_ref[rows(s), :], xv_ref[rows(s), :]], axis=1)
    _compress_core(load, nch, pe_ref, w1_ref, w2_ref, acc0_ref, acc1_ref, k_ref, v_ref)


def _page_specs(n_pages, page, base, row_block):
    def spec(j):
        return pl.BlockSpec((1, 2 * LANES, page),
                            lambda n, pt: (base + pt[n * n_pages + j], row_block, 0))
    return [spec(j) for j in range(n_pages)]


def _compress_sample(cache_t, pt_flat, NB, n_pages, page, base, pe, w1, w2):
    nch = n_pages * page // CMP_STRIDE
    out = jax.ShapeDtypeStruct((NB * nch, LANES), BF16)
    cfull = lambda shape: pl.BlockSpec(shape, lambda n, pt: (0,) * len(shape))
    grid_spec = pltpu.PrefetchScalarGridSpec(
        num_scalar_prefetch=1,
        grid=(NB,),
        in_specs=_page_specs(n_pages, page, base, 0) + [cfull(pe.shape), cfull(w1.shape), cfull(w2.shape)],
        out_specs=[pl.BlockSpec((nch, LANES), lambda n, pt: (n, 0))] * 2,
        scratch_shapes=[pltpu.VMEM((nch, 2 * LANES), F32)] * 2 + [pltpu.VMEM((n_pages * page, LANES), F32)] * 2,
    )
    return pl.pallas_call(
        functools.partial(_compress_sample_kernel, n_pages),
        grid_spec=grid_spec,
        out_shape=[out, out],
        compiler_params=_params(("parallel",)),
        name="compress_sample",
    )(pt_flat, *([cache_t] * n_pages), pe, w1, w2)


def _group_queries(q, kv):
    lane = lax.broadcasted_iota(jnp.int32, (q.shape[0], LANES), 1)
    mine = (lane < HEAD_DIM) if kv == 0 else (lane >= HEAD_DIM)
    parts = []
    for g in range(GQ):
        h = kv * GQ + g
        t = q[:, (h // 2) * LANES:(h // 2 + 1) * LANES]
        if h % 2 != kv:
            t = pltpu.roll(t, HEAD_DIM, 1)
        parts.append(jnp.where(mine, t, 0.0))
    return jnp.concatenate(parts, axis=0).astype(BF16)


def _softmax_parts(s, mask):
    s = jnp.where(mask, s, NEG_INF)
    m = jnp.maximum(jnp.max(s, axis=-1, keepdims=True), M_FLOOR)
    e = jnp.exp(s - m)
    return e.astype(BF16), 1.0 / jnp.maximum(jnp.sum(e, axis=-1, keepdims=True), TINY)


def _select_blocks(score):
    st = score.T
    blk = lax.broadcasted_iota(jnp.int32, st.shape, 0).astype(F32)

    def body(_, carry):
        sc, sel = carry
        m = jnp.max(sc, axis=0, keepdims=True)
        first = jnp.min(jnp.where(sc == m, blk, float(LANES)), axis=0, keepdims=True)
        hit = blk == first
        return jnp.where(hit, -2.0, sc), jnp.where(hit, 1.0, sel)

    _, sel = lax.fori_loop(0, N_SELECT, body, (st, jnp.zeros_like(st)))
    return sel.T


def _block_scores(imp, tq, n_slc):
    blk = lax.broadcasted_iota(jnp.int32, imp.shape, 1)
    cur = tq // SLC_BLOCK
    forced = (blk == 0) | (blk == cur) | (blk == cur - 1)
    allowed = blk * SLC_BLOCK <= tq
    score = jnp.where(forced, FORCE_SCORE, jnp.where(allowed, imp, -1.0))
    return jnp.where(blk < n_slc, score, -3.0)


def _cover(ci, sj):
    return ((ci * CMP_STRIDE < (sj + 1) * SLC_BLOCK) & (ci * CMP_STRIDE + CMP_BLOCK > sj * SLC_BLOCK))


def _nsa_prompt_kernel(q_ref, gate_ref, ck_ref, cv_ref, kvb_ref, et_ref, out_ref, *, seq):
    i = pl.program_id(1)
    R = Q_BLOCK
    s0 = i * R
    q = q_ref[...]
    gate = gate_ref[...]
    ncb = ck_ref.shape[1]
    n_cmp = seq // CMP_STRIDE - 1
    n_slc = seq // SLC_BLOCK
    tok4 = lax.broadcasted_iota(jnp.int32, (GQ * R, 1), 0) % R
    tq4 = s0 + tok4
    tq = s0 + lax.broadcasted_iota(jnp.int32, (R, 1), 0)
    ck = ck_ref[0]
    cv = cv_ref[0]
    ci = lax.broadcasted_iota(jnp.int32, (ncb, LANES), 0)
    sj = lax.broadcasted_iota(jnp.int32, (ncb, LANES), 1)
    cover = jnp.where(_cover(ci, sj) & (ci < n_cmp) & (sj < n_slc), 1.0, 0.0).astype(BF16)
    cmp_i = lax.broadcasted_iota(jnp.int32, (GQ * R, ncb), 1)
    cmp_mask = (cmp_i * CMP_STRIDE + (CMP_BLOCK - 1) <= tq4) & (cmp_i < n_cmp)
    n_tiles = (s0 + R + KEY_TILE - 1) // KEY_TILE
    wstart = pl.multiple_of(jnp.maximum(s0 - WINDOW, 0), R)
    wlen = WINDOW + R
    wpos = wstart + lax.broadcasted_iota(jnp.int32, (GQ * R, wlen), 1)
    wdiff = tq4 - wpos
    wmask = (wdiff >= 0) & (wdiff < WINDOW)
    kcol = lax.broadcasted_iota(jnp.int32, (GQ * R, KEY_TILE), 1)

    qzs = [_group_queries(q, kv) for kv in range(N_KV)]
    kw = kvb_ref[pl.ds(wstart, wlen), 2 * LANES:3 * LANES]
    vw = kvb_ref[pl.ds(wstart, wlen), 3 * LANES:4 * LANES]
    s_cs = [_dot_nt(qz, ck) for qz in qzs]
    s_ws = [_dot_nt(qz, kw) for qz in qzs]
    o_cs, o_ws, scores = [], [], []
    for kv in range(N_KV):
        e_c, inv = _softmax_parts(s_cs[kv], cmp_mask)
        o_cs.append(_dot(e_c, cv) * inv)
        imp4 = _dot(e_c, cover) * inv
        imp = imp4[0:R] + imp4[R:2 * R] + imp4[2 * R:3 * R] + imp4[3 * R:4 * R]
        scores.append(_block_scores(imp, tq, n_slc))
    for kv in range(N_KV):
        e_w, inv = _softmax_parts(s_ws[kv], wmask)
        o_ws.append(_dot(e_w, vw) * inv)
    sel = _select_blocks(jnp.concatenate(scores, axis=0))

    q_augs = []
    for kv in range(N_KV):
        selneg = jnp.where(sel[kv * R:(kv + 1) * R] > 0.5, 0.0, NEG_INF).astype(BF16)
        q_augs.append(jnp.concatenate([qzs[kv], jnp.concatenate([selneg] * GQ, axis=0)], axis=1))

    cw = GQ * R // SEL_SPLIT
    chains = [(kv, c) for kv in range(N_KV) for c in range(SEL_SPLIT)]
    q_t = []
    for kv in range(N_KV):
        qt = q_augs[kv].astype(F32).T.astype(BF16)
        q_t += [qt[:, c * cw:(c + 1) * cw] for c in range(SEL_SPLIT)]
    krow = lax.broadcasted_iota(jnp.int32, (KEY_TILE, cw), 0)
    tq_rows = [s0 + (c * cw + lax.broadcasted_iota(jnp.int32, (1, cw), 1)) % R for _, c in chains]

    def tile_step(off, carry, diagonal):
        k_aug = jnp.concatenate([kvb_ref[pl.ds(off, KEY_TILE), 0:LANES],
                                 et_ref[pl.ds(off, KEY_TILE), :]], axis=1)
        vt = kvb_ref[pl.ds(off, KEY_TILE), LANES:2 * LANES]
        scores_t = [_dot(k_aug, qt) for qt in q_t]
        out = []
        for s, tq_row, (m, l, acc) in zip(scores_t, tq_rows, carry):
            if diagonal:
                s = jnp.where(off + krow <= tq_row, s, NEG_INF)
            m_new = jnp.maximum(m, jnp.max(s, axis=0, keepdims=True))
            alpha = jnp.exp(m - m_new)
            p = jnp.exp(s - m_new)
            l = alpha * l + jnp.sum(p, axis=0, keepdims=True)
            pv = lax.dot_general(vt, p.astype(BF16), (((0,), (0,)), ((), ())), preferred_element_type=F32)
            out.append((m_new, l, alpha * acc + pv))
        return tuple(out)

    init1 = (jnp.full((1, cw), NEG_INF, F32), jnp.zeros((1, cw), F32), jnp.zeros((LANES, cw), F32))
    carry = lax.fori_loop(
        0, n_tiles - 1, lambda t, c: tile_step(pl.multiple_of(t * KEY_TILE, KEY_TILE), c, False),
        (init1,) * len(chains))
    carry = tile_step(pl.multiple_of((n_tiles - 1) * KEY_TILE, KEY_TILE), carry, True)

    for kv in range(N_KV):
        qz, o_c = qzs[kv], o_cs[kv]
        parts = [acc / jnp.maximum(l, TINY) for _, l, acc in carry[kv * SEL_SPLIT:(kv + 1) * SEL_SPLIT]]
        o_s = jnp.concatenate(parts, axis=1).T

        o_w = o_ws[kv]

        for g in range(GQ):
            h = kv * GQ + g
            rows = slice(g * R, (g + 1) * R)
            lanes = slice(kv * HEAD_DIM, (kv + 1) * HEAD_DIM)
            o = (gate[:, 3 * h:3 * h + 1] * o_c[rows, lanes]
                 + gate[:, 3 * h + 1:3 * h + 2] * o_s[rows, lanes]
                 + gate[:, 3 * h + 2:3 * h + 3] * o_w[rows, lanes])
            out_ref[:, h * HEAD_DIM:(h + 1) * HEAD_DIM] = o


def _nsa_prompt(q, gate, ck, cv, kvb, emat, B, S):
    nq = S // Q_BLOCK
    ncb = ck.shape[1]
    rowblk = lambda w_: pl.BlockSpec((Q_BLOCK, w_), lambda b, i: (b * nq + i, 0))
    return pl.pallas_call(
        functools.partial(_nsa_prompt_kernel, seq=S),
        grid=(B, nq),
        in_specs=[rowblk(Q_W), rowblk(GATE_PAD),
                  pl.BlockSpec((1, ncb, LANES), lambda b, i: (b, 0, 0)),
                  pl.BlockSpec((1, ncb, LANES), lambda b, i: (b, 0, 0)),
                  pl.BlockSpec((S, 4 * LANES), lambda b, i: (b, 0)),
                  pl.BlockSpec(emat.shape, lambda b, i: (0, 0))],
        out_specs=rowblk(Q_W),
        out_shape=jax.ShapeDtypeStruct((B * S, Q_W), F32),
        compiler_params=_params(("parallel", "arbitrary")),
        name="nsa_prompt",
    )(q, gate, ck, cv, kvb, emat)


def _nsa_sample_select_kernel(q_ref, gate_ref, ck_ref, cv_ref, sel_ref, oc_ref, *, past_len, tn, ncb):
    R = q_ref.shape[0]
    q = q_ref[...]
    gate = gate_ref[...]
    total = past_len + tn
    n_cmp = total // CMP_STRIDE - 1
    n_slc = -(-total // SLC_BLOCK)
    ncol = ck_ref.shape[0]
    row4 = lax.broadcasted_iota(jnp.int32, (GQ * R, 1), 0) % R
    tq4 = past_len + row4 % tn
    tq = past_len + lax.broadcasted_iota(jnp.int32, (R, 1), 0) % tn
    col = lax.broadcasted_iota(jnp.int32, (GQ * R, ncol), 1)
    ci = col % ncb
    cmp_mask = ((col // ncb == row4 // tn) & (ci * CMP_STRIDE + (CMP_BLOCK - 1) <= tq4) & (ci < n_cmp))
    cr = lax.broadcasted_iota(jnp.int32, (ncol, LANES), 0) % ncb
    sj = lax.broadcasted_iota(jnp.int32, (ncol, LANES), 1)
    cover = jnp.where(_cover(cr, sj) & (cr < n_cmp) & (sj < n_slc), 1.0, 0.0).astype(BF16)
    ck = ck_ref[...]
    cv = cv_ref[...]
    scores = []
    for kv in range(N_KV):
        qz = _group_queries(q, kv)
        e_c, inv = _softmax_parts(_dot_nt(qz, ck), cmp_mask)
        o_c = _dot(e_c, cv) * inv
        imp4 = _dot(e_c, cover) * inv
        imp = imp4[0:R] + imp4[R:2 * R] + imp4[2 * R:3 * R] + imp4[3 * R:4 * R]
        scores.append(_block_scores(imp, tq, n_slc))
        for g in range(GQ):
            h = kv * GQ + g
            oc_ref[:, h * HEAD_DIM:(h + 1) * HEAD_DIM] = (
                gate[:, 3 * h:3 * h + 1] * o_c[g * R:(g + 1) * R, kv * HEAD_DIM:(kv + 1) * HEAD_DIM])
    sel = _select_blocks(jnp.concatenate(scores, axis=0))
    for kv in range(N_KV):
        sel_ref[:, kv * LANES:(kv + 1) * LANES] = jnp.where(sel[kv * R:(kv + 1) * R] > 0.5, 0.0, NEG_INF)


def _nsa_sample_select(q, gate, ck, cv, NB, tn, past_len, ncb):
    R = NSA_SEQ_GROUP * tn
    rowblk = lambda w_: pl.BlockSpec((R, w_), lambda i: (i, 0))
    cblk = pl.BlockSpec((NSA_SEQ_GROUP * ncb, LANES), lambda i: (i, 0))
    return pl.pallas_call(
        functools.partial(_nsa_sample_select_kernel, past_len=past_len, tn=tn, ncb=ncb),
        grid=(NB // NSA_SEQ_GROUP,),
        in_specs=[rowblk(Q_W), rowblk(GATE_PAD), cblk, cblk],
        out_specs=[rowblk(2 * LANES), rowblk(Q_W)],
        out_shape=[jax.ShapeDtypeStruct((NB * tn, 2 * LANES), F32),
                   jax.ShapeDtypeStruct((NB * tn, Q_W), F32)],
        compiler_params=_params(("parallel",)),
        name="nsa_sample_select",
    )(q, gate, ck, cv)


def _nsa_sample_attend_kernel(n_pages, pt_ref, *refs, past_len, tn):
    pages = refs[:n_pages]
    q_ref, gate_ref, sel_ref, oc_ref, kvn_ref, win_ref, e_ref, out_ref = refs[n_pages:]
    R = tn
    q = q_ref[...]
    gate = gate_ref[...]
    kvn = kvn_ref[...]
    wb = win_ref.shape[2]
    new_blk = past_len // SLC_BLOCK
    rows_all = N_KV * GQ * R
    tok4 = lax.broadcasted_iota(jnp.int32, (rows_all, 1), 0) % R
    newcol = lax.broadcasted_iota(jnp.int32, (rows_all, LANES), 1)
    causal_new = newcol <= tok4
    kt_past = jnp.concatenate([p[0, 0:LANES, :] for p in pages], axis=1).astype(BF16)
    vt_past = jnp.concatenate([p[0, LANES:2 * LANES, :] for p in pages], axis=1).astype(BF16)
    pad = jnp.zeros((LANES - tn, KV_W), F32)
    kvn = jnp.concatenate([kvn, pad], axis=0)
    k_new = kvn[:, 2 * LANES:3 * LANES].astype(BF16)
    v_new = kvn[:, 3 * LANES:4 * LANES].astype(BF16)
    kwt = win_ref[0, 0:LANES, :].astype(BF16)
    vwt = win_ref[0, LANES:2 * LANES, :].astype(BF16)
    kw_new = kvn[:, 4 * LANES:5 * LANES].astype(BF16)
    vw_new = kvn[:, 5 * LANES:6 * LANES].astype(BF16)
    wcol = lax.broadcasted_iota(jnp.int32, (rows_all, wb), 1)
    wdiff = wb + tok4 - wcol
    wmask = (wdiff >= 0) & (wdiff < WINDOW) & (wcol >= wb - past_len)

    qz = jnp.concatenate([_group_queries(q, kv) for kv in range(N_KV)], axis=0)
    sel4 = jnp.concatenate([sel_ref[:, kv * LANES:(kv + 1) * LANES] for kv in range(N_KV) for _ in range(GQ)],
                           axis=0)
    s = _dot(qz, kt_past) + _dot(sel4.astype(BF16), e_ref[...])
    s_new = jnp.where(causal_new, _dot_nt(qz, k_new) + sel4[:, new_blk:new_blk + 1], NEG_INF)
    s_w = jnp.where(wmask, _dot(qz, kwt), NEG_INF)
    s_wn = jnp.where(causal_new, _dot_nt(qz, kw_new), NEG_INF)

    m = jnp.maximum(jnp.max(s, axis=-1, keepdims=True), jnp.max(s_new, axis=-1, keepdims=True))
    e = jnp.exp(s - m)
    e_new = jnp.exp(s_new - m)
    l = jnp.sum(e, axis=-1, keepdims=True) + jnp.sum(e_new, axis=-1, keepdims=True)
    o_s_all = (_dot_nt(e.astype(BF16), vt_past) + _dot(e_new.astype(BF16), v_new)) / jnp.maximum(l, TINY)

    m = jnp.maximum(jnp.max(s_w, axis=-1, keepdims=True), jnp.max(s_wn, axis=-1, keepdims=True))
    e = jnp.exp(s_w - m)
    e_new = jnp.exp(s_wn - m)
    l = jnp.sum(e, axis=-1, keepdims=True) + jnp.sum(e_new, axis=-1, keepdims=True)
    o_w_all = (_dot_nt(e.astype(BF16), vwt) + _dot(e_new.astype(BF16), vw_new)) / jnp.maximum(l, TINY)

    for kv in range(N_KV):
        o_s = o_s_all[kv * GQ * R:(kv + 1) * GQ * R]
        o_w = o_w_all[kv * GQ * R:(kv + 1) * GQ * R]
        for g in range(GQ):
            h = kv * GQ + g
            rows = slice(g * R, (g + 1) * R)
            lanes = slice(kv * HEAD_DIM, (kv + 1) * HEAD_DIM)
            hl = slice(h * HEAD_DIM, (h + 1) * HEAD_DIM)
            out_ref[:, hl] = (oc_ref[:, hl] + gate[:, 3 * h + 1:3 * h + 2] * o_s[rows, lanes]
                              + gate[:, 3 * h + 2:3 * h + 3] * o_w[rows, lanes])


def _nsa_sample_attend(cache, win, pt_flat, q, gate, selneg, ocg, kvn, emat,
                       NB, tn, n_pages, page, base, win_base):
    past_len = n_pages * page
    wb = win.shape[2]
    rowblk = lambda w_: pl.BlockSpec((tn, w_), lambda n, pt: (n, 0))
    grid_spec = pltpu.PrefetchScalarGridSpec(
        num_scalar_prefetch=1,
        grid=(NB,),
        in_specs=_page_specs(n_pages, page, base, 1) + [
            rowblk(Q_W), rowblk(GATE_PAD), rowblk(2 * LANES), rowblk(Q_W), rowblk(KV_W),
            pl.BlockSpec((1, 2 * LANES, wb), lambda n, pt: (win_base + n, 0, 0)),
            pl.BlockSpec(emat.shape, lambda n, pt: (0, 0))],
        out_specs=rowblk(Q_W),
    )
    return pl.pallas_call(
        functools.partial(_nsa_sample_attend_kernel, n_pages, past_len=past_len, tn=tn),
        grid_spec=grid_spec,
        out_shape=jax.ShapeDtypeStruct((NB * tn, Q_W), F32),
        compiler_params=_params(("parallel",)),
        name="nsa_sample_attend",
    )(pt_flat, *([cache] * n_pages), q, gate, selneg, ocg, kvn, win, emat)


def _ffn(x1, gf_ref, w1_ref, w3_ref, w2_ref, hb_ref, acc_ref):
    hb_ref[...] = _rms(x1, gf_ref[...]).astype(BF16)
    acc_ref[...] = x1

    def body(c, carry):
        off = pl.multiple_of(c * FF_CHUNK, FF_CHUNK)
        h = hb_ref[...]
        a = _dot(h, w1_ref[:, pl.ds(off, FF_CHUNK)])
        b = _dot(h, w3_ref[:, pl.ds(off, FF_CHUNK)])
        gact = (a * _sigmoid(a) * b).astype(BF16)
        acc_ref[...] += _dot(gact, w2_ref[pl.ds(off, FF_CHUNK), :])
        return carry

    lax.fori_loop(0, D_FF // FF_CHUNK, body, 0)
    return acc_ref[...]


def _finish(x2, final, gfin_ref, out_ref):
    out_ref[...] = _rms(x2, gfin_ref[...]) if final else x2


def _even_tail_kernel(x_ref, attn_ref, u_ref, vn_ref, wmix_ref, bmix_ref, wo_ref,
                      gf_ref, w1_ref, w3_ref, w2_ref, gfin_ref, out_ref,
                      sgu_ref, hb_ref, acc_ref, *, chunk, final):
    tm = x_ref.shape[0]
    r = lax.broadcasted_iota(jnp.int32, (SGU_CHUNK, SGU_CHUNK), 0)
    c = lax.broadcasted_iota(jnp.int32, (SGU_CHUNK, SGU_CHUNK), 1)
    causal = (c <= r) & (r // chunk == c // chunk)
    for g in range(SGU_GROUPS):
        w = jnp.where(causal, wmix_ref[g], 0.0).astype(BF16)
        b = bmix_ref[g]
        lanes = slice(g * SGU_DIM, (g + 1) * SGU_DIM)
        for k in range(tm // SGU_CHUNK):
            rows = slice(k * SGU_CHUNK, (k + 1) * SGU_CHUNK)
            mix = _dot(w, vn_ref[rows, lanes].astype(BF16)) + b
            sgu_ref[rows, lanes] = (u_ref[rows, lanes] * mix).astype(BF16)
    x1 = (x_ref[...] + _dot(attn_ref[...].astype(BF16), wo_ref[0:Q_W, :])
          + _dot(sgu_ref[...], wo_ref[Q_W:Q_W + U_W, :]))
    _finish(_ffn(x1, gf_ref, w1_ref, w3_ref, w2_ref, hb_ref, acc_ref), final, gfin_ref, out_ref)


def _odd_tail_kernel(x_ref, z_ref, wa_ref, wb_ref, gf_ref, w1_ref, w3_ref, w2_ref, gfin_ref, out_ref,
                     hb_ref, acc_ref, *, final):
    z = z_ref[...]
    x1 = x_ref[...] + _dot(z, wa_ref[...]) * _sigmoid(_dot(z, wb_ref[...]))
    _finish(_ffn(x1, gf_ref, w1_ref, w3_ref, w2_ref, hb_ref, acc_ref), final, gfin_ref, out_ref)


def _ffn_specs():
    return [_full((1, D_MODEL)), _full((D_MODEL, D_FF)), _full((D_MODEL, D_FF)), _full((D_FF, D_MODEL)),
            _full((1, D_MODEL))]


def _even_tail(x, attn, u, vn, wmix, bmix, wo, gf, w1, w3, w2, gfin, chunk, final):
    T = x.shape[0]
    tm = min(TOKEN_TILE, T)
    row = lambda w_: pl.BlockSpec((tm, w_), lambda i: (i, 0))
    return pl.pallas_call(
        functools.partial(_even_tail_kernel, chunk=chunk, final=final),
        grid=(T // tm,),
        in_specs=[row(D_MODEL), row(Q_W), row(U_W), row(U_W), _full(wmix.shape), _full(bmix.shape),
                  _full(wo.shape)] + _ffn_specs(),
        out_specs=row(D_MODEL),
        out_shape=jax.ShapeDtypeStruct((T, D_MODEL), F32),
        scratch_shapes=[pltpu.VMEM((tm, U_W), BF16), pltpu.VMEM((tm, D_MODEL), BF16),
                        pltpu.VMEM((tm, D_MODEL), F32)],
        compiler_params=_params(("parallel",)),
        name="even_tail",
    )(x, attn, u, vn, wmix, bmix, wo, gf, w1, w3, w2, gfin)


def _odd_tail(x, z, wa, wb, gf, w1, w3, w2, gfin, final):
    T = x.shape[0]
    tm = min(TOKEN_TILE, T)
    row = lambda w_: pl.BlockSpec((tm, w_), lambda i: (i, 0))
    return pl.pallas_call(
        functools.partial(_odd_tail_kernel, final=final),
        grid=(T // tm,),
        in_specs=[row(D_MODEL), row(D_MODEL), _full(wa.shape), _full(wb.shape)] + _ffn_specs(),
        out_specs=row(D_MODEL),
        out_shape=jax.ShapeDtypeStruct((T, D_MODEL), F32),
        scratch_shapes=[pltpu.VMEM((tm, D_MODEL), BF16), pltpu.VMEM((tm, D_MODEL), F32)],
        compiler_params=_params(("parallel",)),
        name="odd_tail",
    )(x, z, wa, wb, gf, w1, w3, w2, gfin)


def _s5_disc_kernel(are_ref, aim_ref, ls_ref, bre_ref, bim_ref, abre_ref, abim_ref, bbre_ref, bbim_ref):
    a_re = are_ref[...]
    a_im = aim_ref[...]
    dt = jnp.exp(ls_ref[...])
    lr = a_re * dt
    li = a_im * dt
    mag = jnp.exp(lr)
    ab_re = mag * jnp.cos(li)
    ab_im = mag * jnp.sin(li)
    den = a_re * a_re + a_im * a_im
    nr = ab_re - 1.0
    cr = (nr * a_re + ab_im * a_im) / den
    cim = (ab_im * a_re - nr * a_im) / den
    b_re = bre_ref[...]
    b_im = bim_ref[...]
    abre_ref[...] = ab_re
    abim_ref[...] = ab_im
    bbre_ref[...] = cr * b_re - cim * b_im
    bbim_ref[...] = cr * b_im + cim * b_re


def _s5_disc(a_re, a_im, log_step, b_re, b_im):
    col = jax.ShapeDtypeStruct((S5_W, 1), F32)
    mat = jax.ShapeDtypeStruct((S5_W, S5_GROUP), F32)
    ls = jnp.broadcast_to(log_step[:, None], (S5_GROUPS, S5_STATE)).reshape(S5_W, 1)
    return pl.pallas_call(
        _s5_disc_kernel,
        out_shape=[col, col, mat, mat],
        name="s5_disc",
    )(a_re.reshape(S5_W, 1), a_im.reshape(S5_W, 1), ls,
      b_re.reshape(S5_W, S5_GROUP), b_im.reshape(S5_W, S5_GROUP))


S5_BLOCKS = 4
S5_BLK_CH = D_MODEL // S5_BLOCKS
S5_BLK_ST = S5_W // S5_BLOCKS


def _s5_input_states(ub, bw_ref, bu_ref):
    for k in range(S5_BLOCKS):
        r = _dot(ub[:, k * S5_BLK_CH:(k + 1) * S5_BLK_CH], bw_ref[k])
        bu_ref[:, k * S5_BLK_ST:(k + 1) * S5_BLK_ST] = r[:, 0:S5_BLK_ST]
        bu_ref[:, S5_W + k * S5_BLK_ST:S5_W + (k + 1) * S5_BLK_ST] = r[:, S5_BLK_ST:2 * S5_BLK_ST]


def _s5_output(u, h_ref, cre_ref, cim_ref, d_ref, store):
    for k in range(S5_BLOCKS):
        hr = h_ref[:, k * S5_BLK_ST:(k + 1) * S5_BLK_ST].astype(BF16)
        hi = h_ref[:, S5_W + k * S5_BLK_ST:S5_W + (k + 1) * S5_BLK_ST].astype(BF16)
        cols = slice(k * S5_BLK_CH, (k + 1) * S5_BLK_CH)
        y = _dot(hr, cre_ref[k]) - _dot(hi, cim_ref[k]) + d_ref[:, cols] * u[:, cols]
        store(cols, jax.nn.gelu(y).astype(BF16))


def _s5_prompt_kernel(x_ref, g_ref, ab_ref, bw_ref, cre_ref, cim_ref, d_ref,
                      z_ref, hre_ref, him_ref, bu_ref, st_ref, pw_ref, up_ref):
    c = pl.program_id(1)
    tc = x_ref.shape[0]
    seg = tc // S5_SEGMENTS
    ntile = S5_W // LANES

    @pl.when(c == 0)
    def _():
        st_ref[...] = jnp.zeros_like(st_ref)
        ar = ab_ref[0:1, :]
        ai = ab_ref[1:2, :]

        def put(t, pr, pi):
            for j in range(ntile):
                pw_ref[j, pl.ds(t, 1), :] = pr[:, j * LANES:(j + 1) * LANES]
                pw_ref[ntile + j, pl.ds(t, 1), :] = pi[:, j * LANES:(j + 1) * LANES]

        def power(t, carry):
            pr, pi = carry
            nr = ar * pr - ai * pi
            ni = ar * pi + ai * pr
            put(t, nr, ni)
            return nr, ni

        put(0, ar, ai)
        lax.fori_loop(1, seg, power, (ar, ai))

    pitch = seg + S5_SEG_PAD
    u = _rms(x_ref[...], g_ref[...])
    up_ref[...] = jnp.zeros_like(up_ref)
    for s in range(S5_SEGMENTS):
        up_ref[s * pitch:s * pitch + seg, :] = u[s * seg:(s + 1) * seg, :]
    up = up_ref[...]
    ub = up.astype(BF16)
    for k in range(S5_BLOCKS):
        r = _dot(ub[:, k * S5_BLK_CH:(k + 1) * S5_BLK_CH], bw_ref[k])
        for j in range(S5_BLK_ST // LANES):
            t = k * (S5_BLK_ST // LANES) + j
            bu_ref[t] = r[:, j * LANES:(j + 1) * LANES]
            bu_ref[ntile + t] = r[:, S5_BLK_ST + j * LANES:S5_BLK_ST + (j + 1) * LANES]

    for t0 in range(0, ntile, S5_TILE_GROUP):
        tiles = range(t0, t0 + S5_TILE_GROUP)
        rep = lambda v: jnp.broadcast_to(v, (S5_SEGMENTS, LANES))
        a = [(rep(ab_ref[0:1, t * LANES:(t + 1) * LANES]), rep(ab_ref[1:2, t * LANES:(t + 1) * LANES]))
             for t in tiles]

        def local(s, hc):
            rows = pl.ds(s, S5_SEGMENTS, stride=pitch)
            out = []
            for (ar, ai), (hr, hi), t in zip(a, hc, tiles):
                nr = ar * hr - ai * hi + bu_ref[t, rows, :]
                ni = ar * hi + ai * hr + bu_ref[ntile + t, rows, :]
                bu_ref[t, rows, :] = nr
                bu_ref[ntile + t, rows, :] = ni
                out.append((nr, ni))
            return tuple(out)

        zero = jnp.zeros((S5_SEGMENTS, LANES), F32)
        ends = lax.fori_loop(0, seg, local, ((zero, zero),) * S5_TILE_GROUP, unroll=4)
        h0 = []
        for (er, ei), t in zip(ends, tiles):
            lanes = slice(t * LANES, (t + 1) * LANES)
            qr = pw_ref[t, seg - 1:seg, :]
            qi = pw_ref[ntile + t, seg - 1:seg, :]
            hr = st_ref[0:1, lanes]
            hi = st_ref[1:2, lanes]
            hrs, his = [], []
            for s in range(S5_SEGMENTS):
                hrs.append(hr)
                his.append(hi)
                hr, hi = (er[s:s + 1] + qr * hr - qi * hi, ei[s:s + 1] + qr * hi + qi * hr)
            st_ref[0:1, lanes] = hr
            st_ref[1:2, lanes] = hi
            h0.append((jnp.concatenate(hrs, axis=0), jnp.concatenate(his, axis=0)))

        def fix(s, carry_):
            rows = pl.ds(s, S5_SEGMENTS, stride=pitch)
            for (h0r, h0i), t in zip(h0, tiles):
                pr = pw_ref[t, pl.ds(s, 1), :]
                pi = pw_ref[ntile + t, pl.ds(s, 1), :]
                bu_ref[t, rows, :] = bu_ref[t, rows, :] + (pr * h0r - pi * h0i)
                bu_ref[ntile + t, rows, :] = bu_ref[ntile + t, rows, :] + (pr * h0i + pi * h0r)
            return carry_

        lax.fori_loop(0, seg, fix, 0, unroll=2)

    per = S5_BLK_ST // LANES
    for k in range(S5_BLOCKS):
        hr = jnp.concatenate([bu_ref[k * per + j] for j in range(per)], axis=1).astype(BF16)
        hi = jnp.concatenate([bu_ref[ntile + k * per + j] for j in range(per)], axis=1).astype(BF16)
        cols = slice(k * S5_BLK_CH, (k + 1) * S5_BLK_CH)
        y = _dot(hr, cre_ref[k]) - _dot(hi, cim_ref[k]) + d_ref[:, cols] * up[:, cols]
        zp = jax.nn.gelu(y)
        for s in range(S5_SEGMENTS):
            z_ref[s * seg:(s + 1) * seg, cols] = zp[s * pitch:s * pitch + seg, :].astype(BF16)
    hre_ref[0] = st_ref[0:1, :]
    him_ref[0] = st_ref[1:2, :]


def _s5_prompt(x, g, ab, bw, cre, cim, d, B, S):
    tc = min(S5_CHUNK_ROWS, S)
    nc = S // tc
    padded = tc + S5_SEGMENTS * S5_SEG_PAD
    st = jax.ShapeDtypeStruct((B, 1, S5_W), F32)
    return pl.pallas_call(
        _s5_prompt_kernel,
        grid=(B, nc),
        in_specs=[pl.BlockSpec((tc, D_MODEL), lambda b, c: (b * nc + c, 0)), _full((1, D_MODEL)),
                  _full(ab.shape), _full(bw.shape), _full(cre.shape), _full(cim.shape), _full((1, D_MODEL))],
        out_specs=[pl.BlockSpec((tc, D_MODEL), lambda b, c: (b * nc + c, 0)),
                   pl.BlockSpec((1, 1, S5_W), lambda b, c: (b, 0, 0)),
                   pl.BlockSpec((1, 1, S5_W), lambda b, c: (b, 0, 0))],
        out_shape=[jax.ShapeDtypeStruct((B * S, D_MODEL), BF16), st, st],
        scratch_shapes=[pltpu.VMEM((2 * S5_W // LANES, padded, LANES), F32), pltpu.VMEM((2, S5_W), F32),
                        pltpu.VMEM((2 * S5_W // LANES, tc // S5_SEGMENTS, LANES), F32),
                        pltpu.VMEM((padded, D_MODEL), F32)],
        compiler_params=_params(("parallel", "arbitrary")),
        name="s5_prompt",
    )(x, g, ab, bw, cre, cim, d)


def _s5_sample_kernel(x_ref, g_ref, ab_ref, bw_ref, cre_ref, cim_ref, d_ref, h0re_ref, h0im_ref,
                      z_ref, hre_ref, him_ref, bu_ref):
    tn, ns, _ = x_ref.shape
    u = _rms(x_ref[...].reshape(tn * ns, D_MODEL), g_ref[...])
    _s5_input_states(u.astype(BF16), bw_ref, bu_ref)
    ar = ab_ref[0:1, 0:S5_W]
    ai = ab_ref[1:2, 0:S5_W]
    hre_ref[...] = h0re_ref[...]
    him_ref[...] = h0im_ref[...]
    for j in range(tn):
        rows = slice(j * ns, (j + 1) * ns)
        hr = hre_ref[...]
        hi = him_ref[...]
        nr = ar * hr - ai * hi + bu_ref[rows, 0:S5_W]
        ni = ar * hi + ai * hr + bu_ref[rows, S5_W:2 * S5_W]
        hre_ref[...] = nr
        him_ref[...] = ni
        bu_ref[rows, 0:S5_W] = nr
        bu_ref[rows, S5_W:2 * S5_W] = ni

    def store(cols, val):
        z_ref[:, :, cols] = val.reshape(tn, ns, S5_BLK_CH)

    _s5_output(u, bu_ref, cre_ref, cim_ref, d_ref, store)


def _s5_sample(x_t, g, ab, bw, cre, cim, d, h0re, h0im):
    tn, NB, _ = x_t.shape
    ns = min(S5_SEQ_GROUP, NB)
    st = jax.ShapeDtypeStruct((NB, S5_W), F32)
    stb = pl.BlockSpec((ns, S5_W), lambda i: (i, 0))
    xb = pl.BlockSpec((tn, ns, D_MODEL), lambda i: (0, i, 0))
    return pl.pallas_call(
        _s5_sample_kernel,
        grid=(NB // ns,),
        in_specs=[xb, _full((1, D_MODEL)),
                  _full(ab.shape), _full(bw.shape), _full(cre.shape), _full(cim.shape), _full((1, D_MODEL)),
                  stb, stb],
        out_specs=[xb, stb, stb],
        out_shape=[jax.ShapeDtypeStruct((tn, NB, D_MODEL), BF16), st, st],
        scratch_shapes=[pltpu.VMEM((tn * ns, 2 * S5_W), F32)],
        compiler_params=_params(("parallel",)),
        name="s5_sample",
    )(x_t, g, ab, bw, cre, cim, d, h0re, h0im)


def _rope_tables(pos):
    half = ROPE_DIM // 2
    inv_freq = ROPE_THETA ** (-jnp.arange(half, dtype=F32) * (2.0 / ROPE_DIM))
    ang = pos.astype(F32)[:, None] * inv_freq[None, :]
    c, s = jnp.cos(ang), jnp.sin(ang)
    n = pos.shape[0]
    pad = HEAD_DIM - ROPE_DIM
    c64 = jnp.concatenate([c, c, jnp.ones((n, pad), F32)], axis=-1)
    s64 = jnp.concatenate([-s, s, jnp.zeros((n, pad), F32)], axis=-1)
    return jnp.tile(c64, (1, LANES // HEAD_DIM)), jnp.tile(s64, (1, LANES // HEAD_DIM))


def _w_in_layout(w):
    o = Q_W + KV_W
    gate = jnp.pad(w[:, o:o + GATE_W], ((0, 0), (0, GATE_PAD - GATE_W)))
    return jnp.concatenate([w[:, :o], w[:, o + GATE_W:], gate], axis=1).astype(BF16)


def _compress_layout(pe, w1, w2):
    w1r = w1.reshape(2, 2 * CMP_STRIDE, HEAD_DIM, HEAD_DIM)
    w1bd = jnp.zeros((2 * CMP_STRIDE, 2 * LANES, 2 * LANES), BF16)
    w2bd = jnp.zeros((2 * LANES, 2 * LANES), BF16)
    for c in range(2):
        for k in range(N_KV):
            o = (c * N_KV + k) * HEAD_DIM
            w1bd = w1bd.at[:, o:o + HEAD_DIM, o:o + HEAD_DIM].set(w1r[c].astype(BF16))
            w2bd = w2bd.at[o:o + HEAD_DIM, o:o + HEAD_DIM].set(w2[c].astype(BF16))
    per = pe.reshape(2, 2, CMP_STRIDE, HEAD_DIM)
    pel = jnp.broadcast_to(jnp.transpose(per, (1, 2, 0, 3))[:, :, :, None, :],
                           (2, CMP_STRIDE, 2, N_KV, HEAD_DIM)).reshape(2 * CMP_STRIDE, 2 * LANES)
    return pel, w1bd, w2bd


def _s5_layout(bb_re, bb_im, c_re, c_im):
    eye = jnp.eye(S5_GROUPS // S5_BLOCKS, dtype=F32)
    gl = S5_GROUPS // S5_BLOCKS
    bb = jnp.stack([bb_re, bb_im]).reshape(2, S5_BLOCKS, gl, S5_STATE, S5_GROUP)
    bw = jnp.einsum('rkgpc,gh->kgcrhp', bb, eye).reshape(S5_BLOCKS, S5_BLK_CH, 2 * S5_BLK_ST).astype(BF16)

    def cl(c):
        c = c.reshape(S5_BLOCKS, gl, S5_GROUP, S5_STATE)
        return jnp.einsum('kgcp,gh->kgphc', c, eye).reshape(S5_BLOCKS, S5_BLK_ST, S5_BLK_CH).astype(BF16)

    return bw, cl(c_re), cl(c_im)


def _block_onehot(n_keys):
    blk = jnp.arange(LANES)[:, None]
    key = jnp.arange(n_keys)[None, :]
    return (key // SLC_BLOCK == blk).astype(BF16)


def kernel(x_prompt, x_sample, cache_nsa_kv, cache_nsa_win, state_s5_re, state_s5_im, page_table, norm_mix, norm_ffn, norm_final, w_in, w_out, cmp_pe, cmp_w1, cmp_w2, sgu_ln_g, sgu_ln_b, sgu_w, sgu_b, s5_a_re, s5_a_im, s5_log_step, s5_b_re, s5_b_im, s5_c_re, s5_c_im, s5_d, glu_w_a, glu_w_b, ffn_w1, ffn_w3, ffn_w2):
    B, S, _ = x_prompt.shape
    NB, tn, _ = x_sample.shape
    n_even, pool, page = cache_nsa_kv.shape[:3]
    n_pages = page_table.shape[1]
    past_len = n_pages * page
    wb = cache_nsa_win.shape[2]
    assert S % KEY_TILE == 0 and S // SLC_BLOCK <= LANES and S >= WINDOW + Q_BLOCK
    assert past_len % SLC_BLOCK == 0 and tn <= CMP_STRIDE and SGU_CHUNK % tn == 0
    assert NB % NSA_SEQ_GROUP == 0 and -(-(past_len + tn) // SLC_BLOCK) <= LANES

    xp = x_prompt.reshape(B * S, D_MODEL)
    xs = x_sample.reshape(NB * tn, D_MODEL)
    cache = jnp.transpose(cache_nsa_kv, (0, 1, 3, 4, 5, 2)).reshape(n_even * pool, 4 * LANES, page)
    win = jnp.transpose(cache_nsa_win, (0, 1, 3, 4, 5, 2)).reshape(n_even * NB, 2 * LANES, wb)
    pt_flat = page_table.reshape(-1).astype(jnp.int32)
    cos_p, sin_p = _rope_tables(jnp.arange(S))
    cos_s, sin_s = _rope_tables(jnp.tile(past_len + jnp.arange(tn), NB))
    e_prompt = _block_onehot(S).T
    e_sample = _block_onehot(past_len)
    ncb_s = past_len // CMP_STRIDE
    tm_p = min(TOKEN_TILE, S)
    tm_s = min(TOKEN_TILE, NB * tn)
    row = lambda a: a.reshape(1, -1)

    kv_p, kv_s, win_p, win_s, sgu_v_s = [], [], [], [], []
    s5p_re, s5p_im, s5s_re, s5s_im = [], [], [], []
    for layer in range(DEPTH):
        final = layer == DEPTH - 1
        ffn = (row(norm_ffn[layer]), ffn_w1[layer].astype(BF16), ffn_w3[layer].astype(BF16),
               ffn_w2[layer].astype(BF16), row(norm_final))
        gmix = row(norm_mix[layer])
        if layer % 2 == 0:
            e = layer // 2
            wi = _w_in_layout(w_in[e])
            lng, lnb = row(sgu_ln_g[e]), row(sgu_ln_b[e])
            pel, w1bd, w2bd = _compress_layout(cmp_pe[e], cmp_w1[e], cmp_w2[e])
            qp, kvp, kvbp, gp, up, vnp = _inproj(xp, gmix, wi, cos_p, sin_p, lng, lnb, tm_p)
            qs, kvs, _, gs, us, vns = _inproj(xs, gmix, wi, cos_s, sin_s, lng, lnb, tm_s)
            ckp, cvp = _compress_prompt(kvp, B, S, pel, w1bd, w2bd)
            ap = _nsa_prompt(qp, gp, ckp, cvp, kvbp, e_prompt, B, S)
            cks, cvs = _compress_sample(cache, pt_flat, NB, n_pages, page, e * pool, pel, w1bd, w2bd)
            selneg, ocg = _nsa_sample_select(qs, gs, cks, cvs, NB, tn, past_len, ncb_s)
            a_s = _nsa_sample_attend(cache, win, pt_flat, qs, gs, selneg, ocg, kvs, e_sample,
                                     NB, tn, n_pages, page, e * pool, e * NB)
            wo = w_out[e].astype(BF16)
            bmix_p = sgu_b[e][:, :, None]
            reps = SGU_CHUNK // tn
            wmix_s = jnp.tile(sgu_w[e][:, :tn, :tn], (1, reps, reps))
            bmix_s = jnp.tile(sgu_b[e][:, :tn], (1, reps))[:, :, None]
            xp = _even_tail(xp, ap, up, vnp, sgu_w[e], bmix_p, wo, *ffn, chunk=SGU_CHUNK, final=final)
            xs = _even_tail(xs, a_s, us, vns, wmix_s, bmix_s, wo, *ffn, chunk=tn, final=final)
            kvp5 = kvp.reshape(B, S, 6, N_KV, HEAD_DIM)
            kvs5 = kvs.reshape(NB, tn, 6, N_KV, HEAD_DIM)
            kv_p.append(kvp5[:, :, 0:4])
            kv_s.append(kvs5[:, :, 0:4])
            win_p.append(kvp5[:, S - min(WINDOW, S):, 4:6])
            win_s.append(kvs5[:, :, 4:6])
            sgu_v_s.append(vns.reshape(NB, tn, SGU_GROUPS, SGU_DIM))
        else:
            o = layer // 2
            ab_re, ab_im, bb_re, bb_im = _s5_disc(s5_a_re[o], s5_a_im[o], s5_log_step[o], s5_b_re[o], s5_b_im[o])
            ab = jnp.concatenate([ab_re.reshape(1, S5_W), ab_im.reshape(1, S5_W)], axis=0)
            bw, cre, cim = _s5_layout(bb_re.reshape(S5_GROUPS, S5_STATE, S5_GROUP),
                                      bb_im.reshape(S5_GROUPS, S5_STATE, S5_GROUP), s5_c_re[o], s5_c_im[o])
            d = row(s5_d[o])
            zp, hpr, hpi = _s5_prompt(xp, gmix, ab, bw, cre, cim, d, B, S)
            xs_t = jnp.transpose(xs.reshape(NB, tn, D_MODEL), (1, 0, 2))
            zs_t, hsr, hsi = _s5_sample(xs_t, gmix, ab, bw, cre, cim, d,
                                        state_s5_re[o].reshape(NB, S5_W), state_s5_im[o].reshape(NB, S5_W))
            zs = jnp.transpose(zs_t, (1, 0, 2)).reshape(NB * tn, D_MODEL)
            wa, wb_ = glu_w_a[o].astype(BF16), glu_w_b[o].astype(BF16)
            xp = _odd_tail(xp, zp, wa, wb_, *ffn, final=final)
            xs = _odd_tail(xs, zs, wa, wb_, *ffn, final=final)
            s5p_re.append(hpr.reshape(B, S5_GROUPS, S5_STATE))
            s5p_im.append(hpi.reshape(B, S5_GROUPS, S5_STATE))
            s5s_re.append(hsr.reshape(NB, S5_GROUPS, S5_STATE))
            s5s_im.append(hsi.reshape(NB, S5_GROUPS, S5_STATE))
    return (xp.reshape(B, S, D_MODEL), xs.reshape(NB, tn, D_MODEL), jnp.stack(kv_p), jnp.stack(kv_s),
            jnp.stack(win_p), jnp.stack(win_s), jnp.stack(sgu_v_s), jnp.stack(s5p_re), jnp.stack(s5p_im),
            jnp.stack(s5s_re), jnp.stack(s5s_im))
```

```python
import functools
import math

import jax
import jax.numpy as jnp
from jax import lax
from jax.experimental import pallas as pl
from jax.experimental.pallas import tpu as pltpu

F32 = jnp.float32
BF16 = jnp.bfloat16

D_MODEL = 1024
DEPTH = 4
N_HEADS = 8
N_KV = 2
GQ = N_HEADS // N_KV
HEAD_DIM = 64
ROPE_DIM = 16
ROPE_THETA = 500000.0
CMP_BLOCK = 32
CMP_STRIDE = 16
SLC_BLOCK = 64
N_SELECT = 16
WINDOW = 512
Q_BLOCK = 128
SGU_GROUPS = 4
SGU_DIM = 128
SGU_CHUNK = 128
Q_W = N_HEADS * HEAD_DIM
KV_W = 6 * N_KV * HEAD_DIM
GATE_W = 3 * N_HEADS
U_W = SGU_GROUPS * SGU_DIM
S5_GROUP = 16
S5_GROUPS = D_MODEL // S5_GROUP
S5_STATE = 64
S5_W = S5_GROUPS * S5_STATE
D_FF = 2816
EPS = 1e-6
NEG_INF = -1e30
TINY = 1e-30
FORCE_SCORE = 1e4

LANES = 128
GATE_PAD = LANES
W_IN_COLS = Q_W + KV_W + 2 * U_W + GATE_PAD
FF_CHUNK = 256
TOKEN_TILE = 512
KEY_TILE = 1024
CMP_PITCH = 20
PICKED = -2.0
SEL_SPLIT = 2
M_FLOOR = -1e29
S5_CHUNK_ROWS = 512
S5_SEQ_GROUP = 32
S5_SEGMENTS = 8
S5_TILE_GROUP = 8
S5_SEG_PAD = 4
NSA_SEQ_GROUP = 16
VMEM_LIMIT = 56 * 1024 * 1024


def _dot(a, b):
    return jnp.dot(a, b, preferred_element_type=F32)


def _dot_nt(a, b):
    return lax.dot_general(a, b, (((1,), (1,)), ((), ())), preferred_element_type=F32)


def _rms(x, g):
    return x * lax.rsqrt(jnp.mean(x * x, axis=-1, keepdims=True) + EPS) * g


def _sigmoid(x):
    return 1.0 / (1.0 + jnp.exp(-x))


def _params(sem):
    return pltpu.CompilerParams(dimension_semantics=sem, vmem_limit_bytes=VMEM_LIMIT)


def _full(shape):
    n = len(shape)
    return pl.BlockSpec(shape, lambda *_: (0,) * n)


def _inproj_kernel(x_ref, g_ref, w_ref, cos_ref, sin_ref, lng_ref, lnb_ref,
                   q_ref, kv_ref, kvb_ref, gate_ref, u_ref, vn_ref):
    x = x_ref[...]
    h = _rms(x, g_ref[...]).astype(BF16)
    cos = cos_ref[...]
    sin = sin_ref[...]
    lane = lax.broadcasted_iota(jnp.int32, cos.shape, 1)
    low = (lane % HEAD_DIM) < (ROPE_DIM // 2)

    def rope(t):
        rot = jnp.where(low, pltpu.roll(t, LANES - ROPE_DIM // 2, 1), pltpu.roll(t, ROPE_DIM // 2, 1))
        return t * cos + rot * sin

    pq = _dot(h, w_ref[:, 0:Q_W])
    for j in range(Q_W // LANES):
        q_ref[:, j * LANES:(j + 1) * LANES] = rope(pq[:, j * LANES:(j + 1) * LANES]) * (HEAD_DIM ** -0.5)
    pkv = _dot(h, w_ref[:, Q_W:Q_W + KV_W])
    for j in range(KV_W // LANES):
        t = pkv[:, j * LANES:(j + 1) * LANES]
        if j % 2 == 0:
            t = rope(t)
        kv_ref[:, j * LANES:(j + 1) * LANES] = t
        if j >= 2:
            kvb_ref[:, (j - 2) * LANES:(j - 1) * LANES] = t.astype(BF16)
    o = Q_W + KV_W
    u_ref[...] = jax.nn.gelu(_dot(h, w_ref[:, o:o + U_W]))
    v = jax.nn.gelu(_dot(h, w_ref[:, o + U_W:o + 2 * U_W]))
    vc = v - jnp.mean(v, axis=-1, keepdims=True)
    var = jnp.mean(vc * vc, axis=-1, keepdims=True)
    vn_ref[...] = vc * lax.rsqrt(var + EPS) * lng_ref[...] + lnb_ref[...]
    gate_ref[...] = _sigmoid(_dot(h, w_ref[:, o + 2 * U_W:o + 2 * U_W + GATE_PAD]))


def _inproj(x, g, w, cos, sin, lng, lnb, tm):
    T = x.shape[0]
    nt = T // tm
    ntab = cos.shape[0] // tm
    row = lambda w_: pl.BlockSpec((tm, w_), lambda i: (i, 0))
    tab = pl.BlockSpec((tm, LANES), lambda i: (i % ntab, 0))
    return pl.pallas_call(
        _inproj_kernel,
        grid=(nt,),
        in_specs=[row(D_MODEL), _full((1, D_MODEL)), _full((D_MODEL, W_IN_COLS)), tab, tab,
                  _full((1, U_W)), _full((1, U_W))],
        out_specs=[row(Q_W), row(KV_W), row(4 * LANES), row(GATE_PAD), row(U_W), row(U_W)],
        out_shape=[jax.ShapeDtypeStruct((T, Q_W), F32), jax.ShapeDtypeStruct((T, KV_W), F32),
                   jax.ShapeDtypeStruct((T, 4 * LANES), BF16), jax.ShapeDtypeStruct((T, GATE_PAD), F32),
                   jax.ShapeDtypeStruct((T, U_W), F32), jax.ShapeDtypeStruct((T, U_W), F32)],
        compiler_params=_params(("parallel",)),
        name="inproj",
    )(x, g, w, cos, sin, lng, lnb)


def _compress_core(load_rows, nch, pe_ref, w1_ref, w2_ref, acc0_ref, acc1_ref, k_ref, v_ref):
    for s in range(CMP_STRIDE):
        xs = load_rows(s)
        for m, acc in ((0, acc0_ref), (1, acc1_ref)):
            idx = m * CMP_STRIDE + s
            part = _dot((xs + pe_ref[idx:idx + 1, :]).astype(BF16), w1_ref[idx])
            if s == 0:
                acc[...] = part
            else:
                acc[...] += part
    hid = acc0_ref[...] + pltpu.roll(acc1_ref[...], nch - 1, 0)
    out = _dot(jax.nn.gelu(hid).astype(BF16), w2_ref[...])
    k_ref[...] = out[:, 0:LANES].astype(BF16).reshape(k_ref.shape)
    v_ref[...] = out[:, LANES:2 * LANES].astype(BF16).reshape(v_ref.shape)


def _compress_prompt_kernel(xk_ref, xv_ref, pe_ref, w1_ref, w2_ref, k_ref, v_ref, acc0_ref, acc1_ref):
    nch = xk_ref.shape[0] // CMP_STRIDE
    rows = lambda s: pl.ds(s, nch, stride=CMP_STRIDE)
    load = lambda s: jnp.concatenate([xk_ref[rows(s), :], xv_ref[rows(s), :]], axis=1)
    _compress_core(load, nch, pe_ref, w1_ref, w2_ref, acc0_ref, acc1_ref, k_ref, v_ref)


def _compress_prompt(kv, B, S, pe, w1, w2):
    nch = S // CMP_STRIDE
    out = jax.ShapeDtypeStruct((B, nch, LANES), BF16)
    return pl.pallas_call(
        _compress_prompt_kernel,
        grid=(B,),
        in_specs=[pl.BlockSpec((S, LANES), lambda b: (b, 0)), pl.BlockSpec((S, LANES), lambda b: (b, 1)),
                  _full(pe.shape), _full(w1.shape), _full(w2.shape)],
        out_specs=[pl.BlockSpec((1, nch, LANES), lambda b: (b, 0, 0))] * 2,
        out_shape=[out, out],
        scratch_shapes=[pltpu.VMEM((nch, 2 * LANES), F32)] * 2,
        compiler_params=_params(("parallel",)),
        name="compress_prompt",
    )(kv, kv, pe, w1, w2)


def _compress_sample_kernel(n_pages, pt_ref, *refs):
    pages = refs[:n_pages]
    pe_ref, w1_ref, w2_ref, k_ref, v_ref, acc0_ref, acc1_ref, xk_ref, xv_ref = refs[n_pages:]
    page = pages[0].shape[2]
    per = page // CMP_STRIDE
    nch = n_pages * per
    for j, p in enumerate(pages):
        xt = p[0].T
        for c in range(per):
            dst = slice((j * per + c) * CMP_PITCH, (j * per + c) * CMP_PITCH + CMP_STRIDE)
            src = slice(c * CMP_STRIDE, (c + 1) * CMP_STRIDE)
            xk_ref[dst, :] = xt[src, 0:LANES]
            xv_ref[dst, :] = xt[src, LANES:2 * LANES]
    rows = lambda s: pl.ds(s, nch, stride=CMP_PITCH)
    load = lambda s: jnp.concatenate([xk_ref[rows(s), :], xv_ref[rows(s), :]], axis=1)
    _compress_core(load, nch, pe_ref, w1_ref, w2_ref, acc0_ref, acc1_ref, k_ref, v_ref)


def _page_specs(n_pages, page, base, row_block):
    def spec(j):
        return pl.BlockSpec((1, 2 * LANES, page),
                            lambda n, pt: (base + pt[n * n_pages + j], row_block, 0))
    return [spec(j) for j in range(n_pages)]


def _compress_sample(cache_t, pt_flat, NB, n_pages, page, base, pe, w1, w2):
    nch = n_pages * page // CMP_STRIDE
    out = jax.ShapeDtypeStruct((NB * nch, LANES), BF16)
    cfull = lambda shape: pl.BlockSpec(shape, lambda n, pt: (0,) * len(shape))
    grid_spec = pltpu.PrefetchScalarGridSpec(
        num_scalar_prefetch=1,
        grid=(NB,),
        in_specs=_page_specs(n_pages, page, base, 0) + [cfull(pe.shape), cfull(w1.shape), cfull(w2.shape)],
        out_specs=[pl.BlockSpec((nch, LANES), lambda n, pt: (n, 0))] * 2,
        scratch_shapes=[pltpu.VMEM((nch, 2 * LANES), F32)] * 2 + [pltpu.VMEM((nch * CMP_PITCH, LANES), F32)] * 2,
    )
    return pl.pallas_call(
        functools.partial(_compress_sample_kernel, n_pages),
        grid_spec=grid_spec,
        out_shape=[out, out],
        compiler_params=_params(("parallel",)),
        name="compress_sample",
    )(pt_flat, *([cache_t] * n_pages), pe, w1, w2)


def _group_queries(q, kv):
    lane = lax.broadcasted_iota(jnp.int32, (q.shape[0], LANES), 1)
    mine = (lane < HEAD_DIM) if kv == 0 else (lane >= HEAD_DIM)
    parts = []
    for g in range(GQ):
        h = kv * GQ + g
        t = q[:, (h // 2) * LANES:(h // 2 + 1) * LANES]
        if h % 2 != kv:
            t = pltpu.roll(t, HEAD_DIM, 1)
        parts.append(jnp.where(mine, t, 0.0))
    return jnp.concatenate(parts, axis=0).astype(BF16)


def _softmax_parts(s, mask):
    s = jnp.where(mask, s, NEG_INF)
    m = jnp.maximum(jnp.max(s, axis=-1, keepdims=True), M_FLOOR)
    e = jnp.exp(s - m)
    return e.astype(BF16), 1.0 / jnp.maximum(jnp.sum(e, axis=-1, keepdims=True), TINY)


def _select_blocks(score):
    st = score.T
    blk = lax.broadcasted_iota(jnp.int32, st.shape, 0).astype(F32)

    def body(_, sc):
        m = jnp.max(sc, axis=0, keepdims=True)
        first = jnp.min(jnp.where(sc == m, blk, float(LANES)), axis=0, keepdims=True)
        return jnp.where(blk == first, PICKED, sc)

    picked = lax.fori_loop(0, N_SELECT, body, st)
    return jnp.where(picked == PICKED, 1.0, 0.0).T


def _block_scores(imp, tq, n_slc):
    blk = lax.broadcasted_iota(jnp.int32, imp.shape, 1)
    cur = tq // SLC_BLOCK
    forced = (blk == 0) | (blk == cur) | (blk == cur - 1)
    allowed = blk * SLC_BLOCK <= tq
    score = jnp.where(forced, FORCE_SCORE, jnp.where(allowed, imp, -1.0))
    return jnp.where(blk < n_slc, score, -3.0)


def _cover(ci, sj):
    return ((ci * CMP_STRIDE < (sj + 1) * SLC_BLOCK) & (ci * CMP_STRIDE + CMP_BLOCK > sj * SLC_BLOCK))


def _nsa_prompt_kernel(q_ref, gate_ref, ck_ref, cv_ref, kvb_ref, et_ref, out_ref, *, seq):
    i = pl.program_id(1)
    R = Q_BLOCK
    s0 = i * R
    q = q_ref[...]
    gate = gate_ref[...]
    ncb = ck_ref.shape[1]
    n_cmp = seq // CMP_STRIDE - 1
    n_slc = seq // SLC_BLOCK
    tok4 = lax.broadcasted_iota(jnp.int32, (GQ * R, 1), 0) % R
    tq4 = s0 + tok4
    tq = s0 + lax.broadcasted_iota(jnp.int32, (R, 1), 0)
    ck = ck_ref[0]
    cv = cv_ref[0]
    ci = lax.broadcasted_iota(jnp.int32, (ncb, LANES), 0)
    sj = lax.broadcasted_iota(jnp.int32, (ncb, LANES), 1)
    cover = jnp.where(_cover(ci, sj) & (ci < n_cmp) & (sj < n_slc), 1.0, 0.0).astype(BF16)
    cmp_i = lax.broadcasted_iota(jnp.int32, (GQ * R, ncb), 1)
    cmp_mask = (cmp_i * CMP_STRIDE + (CMP_BLOCK - 1) <= tq4) & (cmp_i < n_cmp)
    n_tiles = (s0 + R + KEY_TILE - 1) // KEY_TILE
    wstart = pl.multiple_of(jnp.maximum(s0 - WINDOW, 0), R)
    wlen = WINDOW + R
    wpos = wstart + lax.broadcasted_iota(jnp.int32, (GQ * R, wlen), 1)
    wdiff = tq4 - wpos
    wmask = (wdiff >= 0) & (wdiff < WINDOW)
    kcol = lax.broadcasted_iota(jnp.int32, (GQ * R, KEY_TILE), 1)

    qzs = [_group_queries(q, kv) for kv in range(N_KV)]
    kw = kvb_ref[pl.ds(wstart, wlen), 2 * LANES:3 * LANES]
    vw = kvb_ref[pl.ds(wstart, wlen), 3 * LANES:4 * LANES]
    s_cs = [_dot_nt(qz, ck) for qz in qzs]
    s_ws = [_dot_nt(qz, kw) for qz in qzs]
    o_cs, o_ws, scores = [], [], []
    for kv in range(N_KV):
        e_c, inv = _softmax_parts(s_cs[kv], cmp_mask)
        o_cs.append(_dot(e_c, cv) * inv)
        imp4 = _dot(e_c, cover) * inv
        imp = imp4[0:R] + imp4[R:2 * R] + imp4[2 * R:3 * R] + imp4[3 * R:4 * R]
        scores.append(_block_scores(imp, tq, n_slc))
    for kv in range(N_KV):
        e_w, inv = _softmax_parts(s_ws[kv], wmask)
        o_ws.append(_dot(e_w, vw) * inv)
    sel = _select_blocks(jnp.concatenate(scores, axis=0))

    q_augs = []
    for kv in range(N_KV):
        selneg = jnp.where(sel[kv * R:(kv + 1) * R] > 0.5, 0.0, NEG_INF).astype(BF16)
        q_augs.append(jnp.concatenate([qzs[kv], jnp.concatenate([selneg] * GQ, axis=0)], axis=1))

    cw = GQ * R // SEL_SPLIT
    chains = [(kv, c) for kv in range(N_KV) for c in range(SEL_SPLIT)]
    q_t = []
    for kv in range(N_KV):
        qt = q_augs[kv].astype(F32).T.astype(BF16)
        q_t += [qt[:, c * cw:(c + 1) * cw] for c in range(SEL_SPLIT)]
    krow = lax.broadcasted_iota(jnp.int32, (KEY_TILE, cw), 0)
    tq_rows = [s0 + (c * cw + lax.broadcasted_iota(jnp.int32, (1, cw), 1)) % R for _, c in chains]

    def tile_step(off, carry, diagonal):
        k_aug = jnp.concatenate([kvb_ref[pl.ds(off, KEY_TILE), 0:LANES],
                                 et_ref[pl.ds(off, KEY_TILE), :]], axis=1)
        vt = kvb_ref[pl.ds(off, KEY_TILE), LANES:2 * LANES]
        scores_t = [_dot(k_aug, qt) for qt in q_t]
        out = []
        for s, tq_row, (m, l, acc) in zip(scores_t, tq_rows, carry):
            if diagonal:
                s = jnp.where(off + krow <= tq_row, s, NEG_INF)
            m_new = jnp.maximum(m, jnp.max(s, axis=0, keepdims=True))
            alpha = jnp.exp(m - m_new)
            p = jnp.exp(s - m_new)
            l = alpha * l + jnp.sum(p, axis=0, keepdims=True)
            pv = lax.dot_general(vt, p.astype(BF16), (((0,), (0,)), ((), ())), preferred_element_type=F32)
            out.append((m_new, l, alpha * acc + pv))
        return tuple(out)

    init1 = (jnp.full((1, cw), NEG_INF, F32), jnp.zeros((1, cw), F32), jnp.zeros((LANES, cw), F32))
    carry = lax.fori_loop(
        0, n_tiles - 1, lambda t, c: tile_step(pl.multiple_of(t * KEY_TILE, KEY_TILE), c, False),
        (init1,) * len(chains))
    carry = tile_step(pl.multiple_of((n_tiles - 1) * KEY_TILE, KEY_TILE), carry, True)

    for kv in range(N_KV):
        qz, o_c = qzs[kv], o_cs[kv]
        parts = [acc / jnp.maximum(l, TINY) for _, l, acc in carry[kv * SEL_SPLIT:(kv + 1) * SEL_SPLIT]]
        o_s = jnp.concatenate(parts, axis=1).T

        o_w = o_ws[kv]

        for g in range(GQ):
            h = kv * GQ + g
            rows = slice(g * R, (g + 1) * R)
            lanes = slice(kv * HEAD_DIM, (kv + 1) * HEAD_DIM)
            o = (gate[:, 3 * h:3 * h + 1] * o_c[rows, lanes]
                 + gate[:, 3 * h + 1:3 * h + 2] * o_s[rows, lanes]
                 + gate[:, 3 * h + 2:3 * h + 3] * o_w[rows, lanes])
            out_ref[:, h * HEAD_DIM:(h + 1) * HEAD_DIM] = o


def _nsa_prompt(q, gate, ck, cv, kvb, emat, B, S):
    nq = S // Q_BLOCK
    ncb = ck.shape[1]
    rowblk = lambda w_: pl.BlockSpec((Q_BLOCK, w_), lambda b, i: (b * nq + i, 0))
    return pl.pallas_call(
        functools.partial(_nsa_prompt_kernel, seq=S),
        grid=(B, nq),
        in_specs=[rowblk(Q_W), rowblk(GATE_PAD),
                  pl.BlockSpec((1, ncb, LANES), lambda b, i: (b, 0, 0)),
                  pl.BlockSpec((1, ncb, LANES), lambda b, i: (b, 0, 0)),
                  pl.BlockSpec((S, 4 * LANES), lambda b, i: (b, 0)),
                  pl.BlockSpec(emat.shape, lambda b, i: (0, 0))],
        out_specs=rowblk(Q_W),
        out_shape=jax.ShapeDtypeStruct((B * S, Q_W), F32),
        compiler_params=_params(("parallel", "arbitrary")),
        name="nsa_prompt",
    )(q, gate, ck, cv, kvb, emat)


def _nsa_sample_select_kernel(q_ref, gate_ref, ck_ref, cv_ref, sel_ref, oc_ref, *, past_len, tn, ncb):
    R = q_ref.shape[0]
    q = q_ref[...]
    gate = gate_ref[...]
    total = past_len + tn
    n_cmp = total // CMP_STRIDE - 1
    n_slc = -(-total // SLC_BLOCK)
    ncol = ck_ref.shape[0]
    row4 = lax.broadcasted_iota(jnp.int32, (GQ * R, 1), 0) % R
    tq4 = past_len + row4 % tn
    tq = past_len + lax.broadcasted_iota(jnp.int32, (R, 1), 0) % tn
    col = lax.broadcasted_iota(jnp.int32, (GQ * R, ncol), 1)
    ci = col % ncb
    cmp_mask = ((col // ncb == row4 // tn) & (ci * CMP_STRIDE + (CMP_BLOCK - 1) <= tq4) & (ci < n_cmp))
    cr = lax.broadcasted_iota(jnp.int32, (ncol, LANES), 0) % ncb
    sj = lax.broadcasted_iota(jnp.int32, (ncol, LANES), 1)
    cover = jnp.where(_cover(cr, sj) & (cr < n_cmp) & (sj < n_slc), 1.0, 0.0).astype(BF16)
    ck = ck_ref[...]
    cv = cv_ref[...]
    scores = []
    for kv in range(N_KV):
        qz = _group_queries(q, kv)
        e_c, inv = _softmax_parts(_dot_nt(qz, ck), cmp_mask)
        o_c = _dot(e_c, cv) * inv
        imp4 = _dot(e_c, cover) * inv
        imp = imp4[0:R] + imp4[R:2 * R] + imp4[2 * R:3 * R] + imp4[3 * R:4 * R]
        scores.append(_block_scores(imp, tq, n_slc))
        for g in range(GQ):
            h = kv * GQ + g
            oc_ref[:, h * HEAD_DIM:(h + 1) * HEAD_DIM] = (
                gate[:, 3 * h:3 * h + 1] * o_c[g * R:(g + 1) * R, kv * HEAD_DIM:(kv + 1) * HEAD_DIM])
    sel = _select_blocks(jnp.concatenate(scores, axis=0))
    for kv in range(N_KV):
        sel_ref[:, kv * LANES:(kv + 1) * LANES] = jnp.where(sel[kv * R:(kv + 1) * R] > 0.5, 0.0, NEG_INF)


def _nsa_sample_select(q, gate, ck, cv, NB, tn, past_len, ncb):
    R = NSA_SEQ_GROUP * tn
    rowblk = lambda w_: pl.BlockSpec((R, w_), lambda i: (i, 0))
    cblk = pl.BlockSpec((NSA_SEQ_GROUP * ncb, LANES), lambda i: (i, 0))
    return pl.pallas_call(
        functools.partial(_nsa_sample_select_kernel, past_len=past_len, tn=tn, ncb=ncb),
        grid=(NB // NSA_SEQ_GROUP,),
        in_specs=[rowblk(Q_W), rowblk(GATE_PAD), cblk, cblk],
        out_specs=[rowblk(2 * LANES), rowblk(Q_W)],
        out_shape=[jax.ShapeDtypeStruct((NB * tn, 2 * LANES), F32),
                   jax.ShapeDtypeStruct((NB * tn, Q_W), F32)],
        compiler_params=_params(("parallel",)),
        name="nsa_sample_select",
    )(q, gate, ck, cv)


def _nsa_sample_attend_kernel(n_pages, pt_ref, *refs, past_len, tn):
    pages = refs[:n_pages]
    q_ref, gate_ref, sel_ref, oc_ref, kvn_ref, win_ref, e_ref, out_ref = refs[n_pages:]
    R = tn
    q = q_ref[...]
    gate = gate_ref[...]
    kvn = kvn_ref[...]
    wb = win_ref.shape[2]
    new_blk = past_len // SLC_BLOCK
    rows_all = N_KV * GQ * R
    tok4 = lax.broadcasted_iota(jnp.int32, (rows_all, 1), 0) % R
    newcol = lax.broadcasted_iota(jnp.int32, (rows_all, LANES), 1)
    causal_new = newcol <= tok4
    kt_past = jnp.concatenate([p[0, 0:LANES, :] for p in pages], axis=1).astype(BF16)
    vt_past = jnp.concatenate([p[0, LANES:2 * LANES, :] for p in pages], axis=1).astype(BF16)
    pad = jnp.zeros((LANES - tn, KV_W), F32)
    kvn = jnp.concatenate([kvn, pad], axis=0)
    k_new = kvn[:, 2 * LANES:3 * LANES].astype(BF16)
    v_new = kvn[:, 3 * LANES:4 * LANES].astype(BF16)
    kwt = win_ref[0, 0:LANES, :].astype(BF16)
    vwt = win_ref[0, LANES:2 * LANES, :].astype(BF16)
    kw_new = kvn[:, 4 * LANES:5 * LANES].astype(BF16)
    vw_new = kvn[:, 5 * LANES:6 * LANES].astype(BF16)
    wcol = lax.broadcasted_iota(jnp.int32, (rows_all, wb), 1)
    wdiff = wb + tok4 - wcol
    wmask = (wdiff >= 0) & (wdiff < WINDOW) & (wcol >= wb - past_len)

    qz = jnp.concatenate([_group_queries(q, kv) for kv in range(N_KV)], axis=0)
    sel4 = jnp.concatenate([sel_ref[:, kv * LANES:(kv + 1) * LANES] for kv in range(N_KV) for _ in range(GQ)],
                           axis=0)
    s = _dot(qz, kt_past) + _dot(sel4.astype(BF16), e_ref[...])
    s_new = jnp.where(causal_new, _dot_nt(qz, k_new) + sel4[:, new_blk:new_blk + 1], NEG_INF)
    s_w = jnp.where(wmask, _dot(qz, kwt), NEG_INF)
    s_wn = jnp.where(causal_new, _dot_nt(qz, kw_new), NEG_INF)

    m = jnp.maximum(jnp.max(s, axis=-1, keepdims=True), jnp.max(s_new, axis=-1, keepdims=True))
    e = jnp.exp(s - m)
    e_new = jnp.exp(s_new - m)
    l = jnp.sum(e, axis=-1, keepdims=True) + jnp.sum(e_new, axis=-1, keepdims=True)
    o_s_all = (_dot_nt(e.astype(BF16), vt_past) + _dot(e_new.astype(BF16), v_new)) / jnp.maximum(l, TINY)

    m = jnp.maximum(jnp.max(s_w, axis=-1, keepdims=True), jnp.max(s_wn, axis=-1, keepdims=True))
    e = jnp.exp(s_w - m)
    e_new = jnp.exp(s_wn - m)
    l = jnp.sum(e, axis=-1, keepdims=True) + jnp.sum(e_new, axis=-1, keepdims=True)
    o_w_all = (_dot_nt(e.astype(BF16), vwt) + _dot(e_new.astype(BF16), vw_new)) / jnp.maximum(l, TINY)

    for kv in range(N_KV):
        o_s = o_s_all[kv * GQ * R:(kv + 1) * GQ * R]
        o_w = o_w_all[kv * GQ * R:(kv + 1) * GQ * R]
        for g in range(GQ):
            h = kv * GQ + g
            rows = slice(g * R, (g + 1) * R)
            lanes = slice(kv * HEAD_DIM, (kv + 1) * HEAD_DIM)
            hl = slice(h * HEAD_DIM, (h + 1) * HEAD_DIM)
            out_ref[:, hl] = (oc_ref[:, hl] + gate[:, 3 * h + 1:3 * h + 2] * o_s[rows, lanes]
                              + gate[:, 3 * h + 2:3 * h + 3] * o_w[rows, lanes])


def _nsa_sample_attend(cache, win, pt_flat, q, gate, selneg, ocg, kvn, emat,
                       NB, tn, n_pages, page, base, win_base):
    past_len = n_pages * page
    wb = win.shape[2]
    rowblk = lambda w_: pl.BlockSpec((tn, w_), lambda n, pt: (n, 0))
    grid_spec = pltpu.PrefetchScalarGridSpec(
        num_scalar_prefetch=1,
        grid=(NB,),
        in_specs=_page_specs(n_pages, page, base, 1) + [
            rowblk(Q_W), rowblk(GATE_PAD), rowblk(2 * LANES), rowblk(Q_W), rowblk(KV_W),
            pl.BlockSpec((1, 2 * LANES, wb), lambda n, pt: (win_base + n, 0, 0)),
            pl.BlockSpec(emat.shape, lambda n, pt: (0, 0))],
        out_specs=rowblk(Q_W),
    )
    return pl.pallas_call(
        functools.partial(_nsa_sample_attend_kernel, n_pages, past_len=past_len, tn=tn),
        grid_spec=grid_spec,
        out_shape=jax.ShapeDtypeStruct((NB * tn, Q_W), F32),
        compiler_params=_params(("parallel",)),
        name="nsa_sample_attend",
    )(pt_flat, *([cache] * n_pages), q, gate, selneg, ocg, kvn, win, emat)


def _ffn(x1, gf_ref, w1_ref, w3_ref, w2_ref, hb_ref, acc_ref):
    hb_ref[...] = _rms(x1, gf_ref[...]).astype(BF16)
    acc_ref[...] = x1

    def body(c, carry):
        off = pl.multiple_of(c * FF_CHUNK, FF_CHUNK)
        h = hb_ref[...]
        a = _dot(h, w1_ref[:, pl.ds(off, FF_CHUNK)])
        b = _dot(h, w3_ref[:, pl.ds(off, FF_CHUNK)])
        gact = (a * _sigmoid(a) * b).astype(BF16)
        acc_ref[...] += _dot(gact, w2_ref[pl.ds(off, FF_CHUNK), :])
        return carry

    lax.fori_loop(0, D_FF // FF_CHUNK, body, 0)
    return acc_ref[...]


def _finish(x2, final, gfin_ref, out_ref):
    out_ref[...] = _rms(x2, gfin_ref[...]) if final else x2


def _even_tail_kernel(x_ref, attn_ref, u_ref, vn_ref, wmix_ref, bmix_ref, wo_ref,
                      gf_ref, w1_ref, w3_ref, w2_ref, gfin_ref, out_ref,
                      sgu_ref, hb_ref, acc_ref, *, chunk, final):
    tm = x_ref.shape[0]
    r = lax.broadcasted_iota(jnp.int32, (SGU_CHUNK, SGU_CHUNK), 0)
    c = lax.broadcasted_iota(jnp.int32, (SGU_CHUNK, SGU_CHUNK), 1)
    causal = (c <= r) & (r // chunk == c // chunk)
    for g in range(SGU_GROUPS):
        w = jnp.where(causal, wmix_ref[g], 0.0).astype(BF16)
        b = bmix_ref[g]
        lanes = slice(g * SGU_DIM, (g + 1) * SGU_DIM)
        for k in range(tm // SGU_CHUNK):
            rows = slice(k * SGU_CHUNK, (k + 1) * SGU_CHUNK)
            mix = _dot(w, vn_ref[rows, lanes].astype(BF16)) + b
            sgu_ref[rows, lanes] = (u_ref[rows, lanes] * mix).astype(BF16)
    x1 = (x_ref[...] + _dot(attn_ref[...].astype(BF16), wo_ref[0:Q_W, :])
          + _dot(sgu_ref[...], wo_ref[Q_W:Q_W + U_W, :]))
    _finish(_ffn(x1, gf_ref, w1_ref, w3_ref, w2_ref, hb_ref, acc_ref), final, gfin_ref, out_ref)


def _odd_tail_kernel(x_ref, z_ref, wa_ref, wb_ref, gf_ref, w1_ref, w3_ref, w2_ref, gfin_ref, out_ref,
                     hb_ref, acc_ref, *, final):
    z = z_ref[...]
    x1 = x_ref[...] + _dot(z, wa_ref[...]) * _sigmoid(_dot(z, wb_ref[...]))
    _finish(_ffn(x1, gf_ref, w1_ref, w3_ref, w2_ref, hb_ref, acc_ref), final, gfin_ref, out_ref)


def _ffn_specs():
    return [_full((1, D_MODEL)), _full((D_MODEL, D_FF)), _full((D_MODEL, D_FF)), _full((D_FF, D_MODEL)),
            _full((1, D_MODEL))]


def _even_tail(x, attn, u, vn, wmix, bmix, wo, gf, w1, w3, w2, gfin, chunk, final):
    T = x.shape[0]
    tm = min(TOKEN_TILE, T)
    row = lambda w_: pl.BlockSpec((tm, w_), lambda i: (i, 0))
    return pl.pallas_call(
        functools.partial(_even_tail_kernel, chunk=chunk, final=final),
        grid=(T // tm,),
        in_specs=[row(D_MODEL), row(Q_W), row(U_W), row(U_W), _full(wmix.shape), _full(bmix.shape),
                  _full(wo.shape)] + _ffn_specs(),
        out_specs=row(D_MODEL),
        out_shape=jax.ShapeDtypeStruct((T, D_MODEL), F32),
        scratch_shapes=[pltpu.VMEM((tm, U_W), BF16), pltpu.VMEM((tm, D_MODEL), BF16),
                        pltpu.VMEM((tm, D_MODEL), F32)],
        compiler_params=_params(("parallel",)),
        name="even_tail",
    )(x, attn, u, vn, wmix, bmix, wo, gf, w1, w3, w2, gfin)


def _odd_tail(x, z, wa, wb, gf, w1, w3, w2, gfin, final):
    T = x.shape[0]
    tm = min(TOKEN_TILE, T)
    row = lambda w_: pl.BlockSpec((tm, w_), lambda i: (i, 0))
    return pl.pallas_call(
        functools.partial(_odd_tail_kernel, final=final),
        grid=(T // tm,),
        in_specs=[row(D_MODEL), row(D_MODEL), _full(wa.shape), _full(wb.shape)] + _ffn_specs(),
        out_specs=row(D_MODEL),
        out_shape=jax.ShapeDtypeStruct((T, D_MODEL), F32),
        scratch_shapes=[pltpu.VMEM((tm, D_MODEL), BF16), pltpu.VMEM((tm, D_MODEL), F32)],
        compiler_params=_params(("parallel",)),
        name="odd_tail",
    )(x, z, wa, wb, gf, w1, w3, w2, gfin)


def _s5_disc_kernel(are_ref, aim_ref, ls_ref, bre_ref, bim_ref, abre_ref, abim_ref, bbre_ref, bbim_ref):
    a_re = are_ref[...]
    a_im = aim_ref[...]
    dt = jnp.exp(ls_ref[...])
    lr = a_re * dt
    li = a_im * dt
    mag = jnp.exp(lr)
    ab_re = mag * jnp.cos(li)
    ab_im = mag * jnp.sin(li)
    den = a_re * a_re + a_im * a_im
    nr = ab_re - 1.0
    cr = (nr * a_re + ab_im * a_im) / den
    cim = (ab_im * a_re - nr * a_im) / den
    b_re = bre_ref[...]
    b_im = bim_ref[...]
    abre_ref[...] = ab_re
    abim_ref[...] = ab_im
    bbre_ref[...] = cr * b_re - cim * b_im
    bbim_ref[...] = cr * b_im + cim * b_re


def _s5_disc(a_re, a_im, log_step, b_re, b_im):
    col = jax.ShapeDtypeStruct((S5_W, 1), F32)
    mat = jax.ShapeDtypeStruct((S5_W, S5_GROUP), F32)
    ls = jnp.broadcast_to(log_step[:, None], (S5_GROUPS, S5_STATE)).reshape(S5_W, 1)
    return pl.pallas_call(
        _s5_disc_kernel,
        out_shape=[col, col, mat, mat],
        name="s5_disc",
    )(a_re.reshape(S5_W, 1), a_im.reshape(S5_W, 1), ls,
      b_re.reshape(S5_W, S5_GROUP), b_im.reshape(S5_W, S5_GROUP))


S5_BLOCKS = 4
S5_BLK_CH = D_MODEL // S5_BLOCKS
S5_BLK_ST = S5_W // S5_BLOCKS


def _s5_input_states(ub, bw_ref, bu_ref):
    for k in range(S5_BLOCKS):
        r = _dot(ub[:, k * S5_BLK_CH:(k + 1) * S5_BLK_CH], bw_ref[k])
        bu_ref[:, k * S5_BLK_ST:(k + 1) * S5_BLK_ST] = r[:, 0:S5_BLK_ST]
        bu_ref[:, S5_W + k * S5_BLK_ST:S5_W + (k + 1) * S5_BLK_ST] = r[:, S5_BLK_ST:2 * S5_BLK_ST]


def _s5_output(u, h_ref, cre_ref, cim_ref, d_ref, store):
    for k in range(S5_BLOCKS):
        hr = h_ref[:, k * S5_BLK_ST:(k + 1) * S5_BLK_ST].astype(BF16)
        hi = h_ref[:, S5_W + k * S5_BLK_ST:S5_W + (k + 1) * S5_BLK_ST].astype(BF16)
        cols = slice(k * S5_BLK_CH, (k + 1) * S5_BLK_CH)
        y = _dot(hr, cre_ref[k]) - _dot(hi, cim_ref[k]) + d_ref[:, cols] * u[:, cols]
        store(cols, jax.nn.gelu(y).astype(BF16))


def _s5_prompt_kernel(x_ref, g_ref, ab_ref, bw_ref, cre_ref, cim_ref, d_ref,
                      z_ref, hre_ref, him_ref, bu_ref, st_ref, pw_ref, up_ref):
    c = pl.program_id(1)
    tc = x_ref.shape[0]
    seg = tc // S5_SEGMENTS
    ntile = S5_W // LANES

    @pl.when(c == 0)
    def _():
        st_ref[...] = jnp.zeros_like(st_ref)
        ar = ab_ref[0:1, :]
        ai = ab_ref[1:2, :]

        def put(t, pr, pi):
            for j in range(ntile):
                pw_ref[j, pl.ds(t, 1), :] = pr[:, j * LANES:(j + 1) * LANES]
                pw_ref[ntile + j, pl.ds(t, 1), :] = pi[:, j * LANES:(j + 1) * LANES]

        def power(t, carry):
            pr, pi = carry
            nr = ar * pr - ai * pi
            ni = ar * pi + ai * pr
            put(t, nr, ni)
            return nr, ni

        put(0, ar, ai)
        lax.fori_loop(1, seg, power, (ar, ai))

    pitch = seg + S5_SEG_PAD
    u = _rms(x_ref[...], g_ref[...])
    up_ref[...] = jnp.zeros_like(up_ref)
    for s in range(S5_SEGMENTS):
        up_ref[s * pitch:s * pitch + seg, :] = u[s * seg:(s + 1) * seg, :]
    up = up_ref[...]
    ub = up.astype(BF16)
    for k in range(S5_BLOCKS):
        r = _dot(ub[:, k * S5_BLK_CH:(k + 1) * S5_BLK_CH], bw_ref[k])
        for j in range(S5_BLK_ST // LANES):
            t = k * (S5_BLK_ST // LANES) + j
            bu_ref[t] = r[:, j * LANES:(j + 1) * LANES]
            bu_ref[ntile + t] = r[:, S5_BLK_ST + j * LANES:S5_BLK_ST + (j + 1) * LANES]

    for t0 in range(0, ntile, S5_TILE_GROUP):
        tiles = range(t0, t0 + S5_TILE_GROUP)
        rep = lambda v: jnp.broadcast_to(v, (S5_SEGMENTS, LANES))
        a = [(rep(ab_ref[0:1, t * LANES:(t + 1) * LANES]), rep(ab_ref[1:2, t * LANES:(t + 1) * LANES]))
             for t in tiles]

        def local(s, hc):
            rows = pl.ds(s, S5_SEGMENTS, stride=pitch)
            out = []
            for (ar, ai), (hr, hi), t in zip(a, hc, tiles):
                nr = ar * hr - ai * hi + bu_ref[t, rows, :]
                ni = ar * hi + ai * hr + bu_ref[ntile + t, rows, :]
                bu_ref[t, rows, :] = nr
                bu_ref[ntile + t, rows, :] = ni
                out.append((nr, ni))
            return tuple(out)

        zero = jnp.zeros((S5_SEGMENTS, LANES), F32)
        ends = lax.fori_loop(0, seg, local, ((zero, zero),) * S5_TILE_GROUP, unroll=4)
        h0 = []
        for (er, ei), t in zip(ends, tiles):
            lanes = slice(t * LANES, (t + 1) * LANES)
            qr = pw_ref[t, seg - 1:seg, :]
            qi = pw_ref[ntile + t, seg - 1:seg, :]
            hr = st_ref[0:1, lanes]
            hi = st_ref[1:2, lanes]
            hrs, his = [], []
            for s in range(S5_SEGMENTS):
                hrs.append(hr)
                his.append(hi)
                hr, hi = (er[s:s + 1] + qr * hr - qi * hi, ei[s:s + 1] + qr * hi + qi * hr)
            st_ref[0:1, lanes] = hr
            st_ref[1:2, lanes] = hi
            h0.append((jnp.concatenate(hrs, axis=0), jnp.concatenate(his, axis=0)))

        def fix(s, carry_):
            rows = pl.ds(s, S5_SEGMENTS, stride=pitch)
            for (h0r, h0i), t in zip(h0, tiles):
                pr = pw_ref[t, pl.ds(s, 1), :]
                pi = pw_ref[ntile + t, pl.ds(s, 1), :]
                bu_ref[t, rows, :] = bu_ref[t, rows, :] + (pr * h0r - pi * h0i)
                bu_ref[ntile + t, rows, :] = bu_ref[ntile + t, rows, :] + (pr * h0i + pi * h0r)
            return carry_

        lax.fori_loop(0, seg, fix, 0, unroll=2)

    per = S5_BLK_ST // LANES
    for k in range(S5_BLOCKS):
        hr = jnp.concatenate([bu_ref[k * per + j] for j in range(per)], axis=1).astype(BF16)
        hi = jnp.concatenate([bu_ref[ntile + k * per + j] for j in range(per)], axis=1).astype(BF16)
        cols = slice(k * S5_BLK_CH, (k + 1) * S5_BLK_CH)
        y = _dot(hr, cre_ref[k]) - _dot(hi, cim_ref[k]) + d_ref[:, cols] * up[:, cols]
        zp = jax.nn.gelu(y)
        for s in range(S5_SEGMENTS):
            z_ref[s * seg:(s + 1) * seg, cols] = zp[s * pitch:s * pitch + seg, :].astype(BF16)
    hre_ref[0] = st_ref[0:1, :]
    him_ref[0] = st_ref[1:2, :]


def _s5_prompt(x, g, ab, bw, cre, cim, d, B, S):
    tc = min(S5_CHUNK_ROWS, S)
    nc = S // tc
    padded = tc + S5_SEGMENTS * S5_SEG_PAD
    st = jax.ShapeDtypeStruct((B, 1, S5_W), F32)
    return pl.pallas_call(
        _s5_prompt_kernel,
        grid=(B, nc),
        in_specs=[pl.BlockSpec((tc, D_MODEL), lambda b, c: (b * nc + c, 0)), _full((1, D_MODEL)),
                  _full(ab.shape), _full(bw.shape), _full(cre.shape), _full(cim.shape), _full((1, D_MODEL))],
        out_specs=[pl.BlockSpec((tc, D_MODEL), lambda b, c: (b * nc + c, 0)),
                   pl.BlockSpec((1, 1, S5_W), lambda b, c: (b, 0, 0)),
                   pl.BlockSpec((1, 1, S5_W), lambda b, c: (b, 0, 0))],
        out_shape=[jax.ShapeDtypeStruct((B * S, D_MODEL), BF16), st, st],
        scratch_shapes=[pltpu.VMEM((2 * S5_W // LANES, padded, LANES), F32), pltpu.VMEM((2, S5_W), F32),
                        pltpu.VMEM((2 * S5_W // LANES, tc // S5_SEGMENTS, LANES), F32),
                        pltpu.VMEM((padded, D_MODEL), F32)],
        compiler_params=_params(("parallel", "arbitrary")),
        name="s5_prompt",
    )(x, g, ab, bw, cre, cim, d)


def _s5_sample_kernel(x_ref, g_ref, ab_ref, bw_ref, cre_ref, cim_ref, d_ref, h0re_ref, h0im_ref,
                      z_ref, hre_ref, him_ref, bu_ref):
    tn, ns, _ = x_ref.shape
    u = _rms(x_ref[...].reshape(tn * ns, D_MODEL), g_ref[...])
    _s5_input_states(u.astype(BF16), bw_ref, bu_ref)
    ar = ab_ref[0:1, 0:S5_W]
    ai = ab_ref[1:2, 0:S5_W]
    hre_ref[...] = h0re_ref[...]
    him_ref[...] = h0im_ref[...]
    for j in range(tn):
        rows = slice(j * ns, (j + 1) * ns)
        hr = hre_ref[...]
        hi = him_ref[...]
        nr = ar * hr - ai * hi + bu_ref[rows, 0:S5_W]
        ni = ar * hi + ai * hr + bu_ref[rows, S5_W:2 * S5_W]
        hre_ref[...] = nr
        him_ref[...] = ni
        bu_ref[rows, 0:S5_W] = nr
        bu_ref[rows, S5_W:2 * S5_W] = ni

    def store(cols, val):
        z_ref[:, :, cols] = val.reshape(tn, ns, S5_BLK_CH)

    _s5_output(u, bu_ref, cre_ref, cim_ref, d_ref, store)


def _s5_sample(x_t, g, ab, bw, cre, cim, d, h0re, h0im):
    tn, NB, _ = x_t.shape
    ns = min(S5_SEQ_GROUP, NB)
    st = jax.ShapeDtypeStruct((NB, S5_W), F32)
    stb = pl.BlockSpec((ns, S5_W), lambda i: (i, 0))
    xb = pl.BlockSpec((tn, ns, D_MODEL), lambda i: (0, i, 0))
    return pl.pallas_call(
        _s5_sample_kernel,
        grid=(NB // ns,),
        in_specs=[xb, _full((1, D_MODEL)),
                  _full(ab.shape), _full(bw.shape), _full(cre.shape), _full(cim.shape), _full((1, D_MODEL)),
                  stb, stb],
        out_specs=[xb, stb, stb],
        out_shape=[jax.ShapeDtypeStruct((tn, NB, D_MODEL), BF16), st, st],
        scratch_shapes=[pltpu.VMEM((tn * ns, 2 * S5_W), F32)],
        compiler_params=_params(("parallel",)),
        name="s5_sample",
    )(x_t, g, ab, bw, cre, cim, d, h0re, h0im)


def _rope_tables(pos):
    half = ROPE_DIM // 2
    inv_freq = ROPE_THETA ** (-jnp.arange(half, dtype=F32) * (2.0 / ROPE_DIM))
    ang = pos.astype(F32)[:, None] * inv_freq[None, :]
    c, s = jnp.cos(ang), jnp.sin(ang)
    n = pos.shape[0]
    pad = HEAD_DIM - ROPE_DIM
    c64 = jnp.concatenate([c, c, jnp.ones((n, pad), F32)], axis=-1)
    s64 = jnp.concatenate([-s, s, jnp.zeros((n, pad), F32)], axis=-1)
    return jnp.tile(c64, (1, LANES // HEAD_DIM)), jnp.tile(s64, (1, LANES // HEAD_DIM))


def _w_in_layout(w):
    o = Q_W + KV_W
    gate = jnp.pad(w[:, o:o + GATE_W], ((0, 0), (0, GATE_PAD - GATE_W)))
    return jnp.concatenate([w[:, :o], w[:, o + GATE_W:], gate], axis=1).astype(BF16)


def _compress_layout(pe, w1, w2):
    w1r = w1.reshape(2, 2 * CMP_STRIDE, HEAD_DIM, HEAD_DIM)
    w1bd = jnp.zeros((2 * CMP_STRIDE, 2 * LANES, 2 * LANES), BF16)
    w2bd = jnp.zeros((2 * LANES, 2 * LANES), BF16)
    for c in range(2):
        for k in range(N_KV):
            o = (c * N_KV + k) * HEAD_DIM
            w1bd = w1bd.at[:, o:o + HEAD_DIM, o:o + HEAD_DIM].set(w1r[c].astype(BF16))
            w2bd = w2bd.at[o:o + HEAD_DIM, o:o + HEAD_DIM].set(w2[c].astype(BF16))
    per = pe.reshape(2, 2, CMP_STRIDE, HEAD_DIM)
    pel = jnp.broadcast_to(jnp.transpose(per, (1, 2, 0, 3))[:, :, :, None, :],
                           (2, CMP_STRIDE, 2, N_KV, HEAD_DIM)).reshape(2 * CMP_STRIDE, 2 * LANES)
    return pel, w1bd, w2bd


def _s5_layout(bb_re, bb_im, c_re, c_im):
    eye = jnp.eye(S5_GROUPS // S5_BLOCKS, dtype=F32)
    gl = S5_GROUPS // S5_BLOCKS
    bb = jnp.stack([bb_re, bb_im]).reshape(2, S5_BLOCKS, gl, S5_STATE, S5_GROUP)
    bw = jnp.einsum('rkgpc,gh->kgcrhp', bb, eye).reshape(S5_BLOCKS, S5_BLK_CH, 2 * S5_BLK_ST).astype(BF16)

    def cl(c):
        c = c.reshape(S5_BLOCKS, gl, S5_GROUP, S5_STATE)
        return jnp.einsum('kgcp,gh->kgphc', c, eye).reshape(S5_BLOCKS, S5_BLK_ST, S5_BLK_CH).astype(BF16)

    return bw, cl(c_re), cl(c_im)


def _block_onehot(n_keys):
    blk = jnp.arange(LANES)[:, None]
    key = jnp.arange(n_keys)[None, :]
    return (key // SLC_BLOCK == blk).astype(BF16)


def kernel(x_prompt, x_sample, cache_nsa_kv, cache_nsa_win, state_s5_re, state_s5_im, page_table, norm_mix, norm_ffn, norm_final, w_in, w_out, cmp_pe, cmp_w1, cmp_w2, sgu_ln_g, sgu_ln_b, sgu_w, sgu_b, s5_a_re, s5_a_im, s5_log_step, s5_b_re, s5_b_im, s5_c_re, s5_c_im, s5_d, glu_w_a, glu_w_b, ffn_w1, ffn_w3, ffn_w2):
    B, S, _ = x_prompt.shape
    NB, tn, _ = x_sample.shape
    n_even, pool, page = cache_nsa_kv.shape[:3]
    n_pages = page_table.shape[1]
    past_len = n_pages * page
    wb = cache_nsa_win.shape[2]
    assert S % KEY_TILE == 0 and S // SLC_BLOCK <= LANES and S >= WINDOW + Q_BLOCK
    assert past_len % SLC_BLOCK == 0 and tn <= CMP_STRIDE and SGU_CHUNK % tn == 0
    assert NB % NSA_SEQ_GROUP == 0 and -(-(past_len + tn) // SLC_BLOCK) <= LANES

    xp = x_prompt.reshape(B * S, D_MODEL)
    xs = x_sample.reshape(NB * tn, D_MODEL)
    cache = jnp.transpose(cache_nsa_kv, (0, 1, 3, 4, 5, 2)).reshape(n_even * pool, 4 * LANES, page)
    win = jnp.transpose(cache_nsa_win, (0, 1, 3, 4, 5, 2)).reshape(n_even * NB, 2 * LANES, wb)
    pt_flat = page_table.reshape(-1).astype(jnp.int32)
    cos_p, sin_p = _rope_tables(jnp.arange(S))
    cos_s, sin_s = _rope_tables(jnp.tile(past_len + jnp.arange(tn), NB))
    e_prompt = _block_onehot(S).T
    e_sample = _block_onehot(past_len)
    ncb_s = past_len // CMP_STRIDE
    tm_p = min(TOKEN_TILE, S)
    tm_s = min(TOKEN_TILE, NB * tn)
    row = lambda a: a.reshape(1, -1)

    kv_p, kv_s, win_p, win_s, sgu_v_s = [], [], [], [], []
    s5p_re, s5p_im, s5s_re, s5s_im = [], [], [], []
    for layer in range(DEPTH):
        final = layer == DEPTH - 1
        ffn = (row(norm_ffn[layer]), ffn_w1[layer].astype(BF16), ffn_w3[layer].astype(BF16),
               ffn_w2[layer].astype(BF16), row(norm_final))
        gmix = row(norm_mix[layer])
        if layer % 2 == 0:
            e = layer // 2
            wi = _w_in_layout(w_in[e])
            lng, lnb = row(sgu_ln_g[e]), row(sgu_ln_b[e])
            pel, w1bd, w2bd = _compress_layout(cmp_pe[e], cmp_w1[e], cmp_w2[e])
            qp, kvp, kvbp, gp, up, vnp = _inproj(xp, gmix, wi, cos_p, sin_p, lng, lnb, tm_p)
            qs, kvs, _, gs, us, vns = _inproj(xs, gmix, wi, cos_s, sin_s, lng, lnb, tm_s)
            ckp, cvp = _compress_prompt(kvp, B, S, pel, w1bd, w2bd)
            ap = _nsa_prompt(qp, gp, ckp, cvp, kvbp, e_prompt, B, S)
            cks, cvs = _compress_sample(cache, pt_flat, NB, n_pages, page, e * pool, pel, w1bd, w2bd)
            selneg, ocg = _nsa_sample_select(qs, gs, cks, cvs, NB, tn, past_len, ncb_s)
            a_s = _nsa_sample_attend(cache, win, pt_flat, qs, gs, selneg, ocg, kvs, e_sample,
                                     NB, tn, n_pages, page, e * pool, e * NB)
            wo = w_out[e].astype(BF16)
            bmix_p = sgu_b[e][:, :, None]
            reps = SGU_CHUNK // tn
            wmix_s = jnp.tile(sgu_w[e][:, :tn, :tn], (1, reps, reps))
            bmix_s = jnp.tile(sgu_b[e][:, :tn], (1, reps))[:, :, None]
            xp = _even_tail(xp, ap, up, vnp, sgu_w[e], bmix_p, wo, *ffn, chunk=SGU_CHUNK, final=final)
            xs = _even_tail(xs, a_s, us, vns, wmix_s, bmix_s, wo, *ffn, chunk=tn, final=final)
            kvp5 = kvp.reshape(B, S, 6, N_KV, HEAD_DIM)
            kvs5 = kvs.reshape(NB, tn, 6, N_KV, HEAD_DIM)
            kv_p.append(kvp5[:, :, 0:4])
            kv_s.append(kvs5[:, :, 0:4])
            win_p.append(kvp5[:, S - min(WINDOW, S):, 4:6])
            win_s.append(kvs5[:, :, 4:6])
            sgu_v_s.append(vns.reshape(NB, tn, SGU_GROUPS, SGU_DIM))
        else:
            o = layer // 2
            ab_re, ab_im, bb_re, bb_im = _s5_disc(s5_a_re[o], s5_a_im[o], s5_log_step[o], s5_b_re[o], s5_b_im[o])
            ab = jnp.concatenate([ab_re.reshape(1, S5_W), ab_im.reshape(1, S5_W)], axis=0)
            bw, cre, cim = _s5_layout(bb_re.reshape(S5_GROUPS, S5_STATE, S5_GROUP),
                                      bb_im.reshape(S5_GROUPS, S5_STATE, S5_GROUP), s5_c_re[o], s5_c_im[o])
            d = row(s5_d[o])
            zp, hpr, hpi = _s5_prompt(xp, gmix, ab, bw, cre, cim, d, B, S)
            xs_t = jnp.transpose(xs.reshape(NB, tn, D_MODEL), (1, 0, 2))
            zs_t, hsr, hsi = _s5_sample(xs_t, gmix, ab, bw, cre, cim, d,
                                        state_s5_re[o].reshape(NB, S5_W), state_s5_im[o].reshape(NB, S5_W))
            zs = jnp.transpose(zs_t, (1, 0, 2)).reshape(NB * tn, D_MODEL)
            wa, wb_ = glu_w_a[o].astype(BF16), glu_w_b[o].astype(BF16)
            xp = _odd_tail(xp, zp, wa, wb_, *ffn, final=final)
            xs = _odd_tail(xs, zs, wa, wb_, *ffn, final=final)
            s5p_re.append(hpr.reshape(B, S5_GROUPS, S5_STATE))
            s5p_im.append(hpi.reshape(B, S5_GROUPS, S5_STATE))
            s5s_re.append(hsr.reshape(NB, S5_GROUPS, S5_STATE))
            s5s_im.append(hsi.reshape(NB, S5_GROUPS, S5_STATE))
    return (xp.reshape(B, S, D_MODEL), xs.reshape(NB, tn, D_MODEL), jnp.stack(kv_p), jnp.stack(kv_s),
            jnp.stack(win_p), jnp.stack(win_s), jnp.stack(sgu_v_s), jnp.stack(s5p_re), jnp.stack(s5p_im),
            jnp.stack(s5s_re), jnp.stack(s5s_im))
```

```python
import functools
import math

import jax
import jax.numpy as jnp
from jax import lax
from jax.experimental import pallas as pl
from jax.experimental.pallas import tpu as pltpu

F32 = jnp.float32
BF16 = jnp.bfloat16

D_MODEL = 1024
DEPTH = 4
N_HEADS = 8
N_KV = 2
GQ = N_HEADS // N_KV
HEAD_DIM = 64
ROPE_DIM = 16
ROPE_THETA = 500000.0
CMP_BLOCK = 32
CMP_STRIDE = 16
SLC_BLOCK = 64
N_SELECT = 16
WINDOW = 512
Q_BLOCK = 128
SGU_GROUPS = 4
SGU_DIM = 128
SGU_CHUNK = 128
Q_W = N_HEADS * HEAD_DIM
KV_W = 6 * N_KV * HEAD_DIM
GATE_W = 3 * N_HEADS
U_W = SGU_GROUPS * SGU_DIM
S5_GROUP = 16
S5_GROUPS = D_MODEL // S5_GROUP
S5_STATE = 64
S5_W = S5_GROUPS * S5_STATE
D_FF = 2816
EPS = 1e-6
NEG_INF = -1e30
TINY = 1e-30
FORCE_SCORE = 1e4

LANES = 128
GATE_PAD = LANES
W_IN_COLS = Q_W + KV_W + 2 * U_W + GATE_PAD
FF_CHUNK = 1408
TOKEN_TILE = 512
KEY_TILE = 1024
CMP_PITCH = 20
PICKED = -2.0
SEL_SPLIT = 2
M_FLOOR = -1e29
S5_CHUNK_ROWS = 512
S5_SEQ_GROUP = 32
S5_SEGMENTS = 8
S5_TILE_GROUP = 8
S5_SEG_PAD = 4
NSA_SEQ_GROUP = 16
VMEM_LIMIT = 56 * 1024 * 1024


def _dot(a, b):
    return jnp.dot(a, b, preferred_element_type=F32)


def _dot_nt(a, b):
    return lax.dot_general(a, b, (((1,), (1,)), ((), ())), preferred_element_type=F32)


def _rms(x, g):
    return x * lax.rsqrt(jnp.mean(x * x, axis=-1, keepdims=True) + EPS) * g


def _sigmoid(x):
    return 1.0 / (1.0 + jnp.exp(-x))


def _params(sem):
    return pltpu.CompilerParams(dimension_semantics=sem, vmem_limit_bytes=VMEM_LIMIT)


def _full(shape):
    n = len(shape)
    return pl.BlockSpec(shape, lambda *_: (0,) * n)


def _inproj_kernel(x_ref, g_ref, w_ref, cos_ref, sin_ref, lng_ref, lnb_ref,
                   q_ref, kv_ref, kvb_ref, gate_ref, u_ref, vn_ref):
    x = x_ref[...]
    h = _rms(x, g_ref[...]).astype(BF16)
    cos = cos_ref[...]
    sin = sin_ref[...]
    lane = lax.broadcasted_iota(jnp.int32, cos.shape, 1)
    low = (lane % HEAD_DIM) < (ROPE_DIM // 2)

    def rope(t):
        rot = jnp.where(low, pltpu.roll(t, LANES - ROPE_DIM // 2, 1), pltpu.roll(t, ROPE_DIM // 2, 1))
        return t * cos + rot * sin

    pq = _dot(h, w_ref[:, 0:Q_W])
    for j in range(Q_W // LANES):
        q_ref[:, j * LANES:(j + 1) * LANES] = rope(pq[:, j * LANES:(j + 1) * LANES]) * (HEAD_DIM ** -0.5)
    pkv = _dot(h, w_ref[:, Q_W:Q_W + KV_W])
    for j in range(KV_W // LANES):
        t = pkv[:, j * LANES:(j + 1) * LANES]
        if j % 2 == 0:
            t = rope(t)
        kv_ref[:, j * LANES:(j + 1) * LANES] = t
        if j >= 2:
            kvb_ref[:, (j - 2) * LANES:(j - 1) * LANES] = t.astype(BF16)
    o = Q_W + KV_W
    u_ref[...] = jax.nn.gelu(_dot(h, w_ref[:, o:o + U_W]))
    v = jax.nn.gelu(_dot(h, w_ref[:, o + U_W:o + 2 * U_W]))
    vc = v - jnp.mean(v, axis=-1, keepdims=True)
    var = jnp.mean(vc * vc, axis=-1, keepdims=True)
    vn_ref[...] = vc * lax.rsqrt(var + EPS) * lng_ref[...] + lnb_ref[...]
    gate_ref[...] = _sigmoid(_dot(h, w_ref[:, o + 2 * U_W:o + 2 * U_W + GATE_PAD]))


def _inproj(x, g, w, cos, sin, lng, lnb, tm):
    T = x.shape[0]
    nt = T // tm
    ntab = cos.shape[0] // tm
    row = lambda w_: pl.BlockSpec((tm, w_), lambda i: (i, 0))
    tab = pl.BlockSpec((tm, LANES), lambda i: (i % ntab, 0))
    return pl.pallas_call(
        _inproj_kernel,
        grid=(nt,),
        in_specs=[row(D_MODEL), _full((1, D_MODEL)), _full((D_MODEL, W_IN_COLS)), tab, tab,
                  _full((1, U_W)), _full((1, U_W))],
        out_specs=[row(Q_W), row(KV_W), row(4 * LANES), row(GATE_PAD), row(U_W), row(U_W)],
        out_shape=[jax.ShapeDtypeStruct((T, Q_W), F32), jax.ShapeDtypeStruct((T, KV_W), F32),
                   jax.ShapeDtypeStruct((T, 4 * LANES), BF16), jax.ShapeDtypeStruct((T, GATE_PAD), F32),
                   jax.ShapeDtypeStruct((T, U_W), F32), jax.ShapeDtypeStruct((T, U_W), F32)],
        compiler_params=_params(("parallel",)),
        name="inproj",
    )(x, g, w, cos, sin, lng, lnb)


def _compress_core(load_rows, nch, pe_ref, w1_ref, w2_ref, acc0_ref, acc1_ref, k_ref, v_ref):
    for s in range(CMP_STRIDE):
        xs = load_rows(s)
        for m, acc in ((0, acc0_ref), (1, acc1_ref)):
            idx = m * CMP_STRIDE + s
            part = _dot((xs + pe_ref[idx:idx + 1, :]).astype(BF16), w1_ref[idx])
            if s == 0:
                acc[...] = part
            else:
                acc[...] += part
    hid = acc0_ref[...] + pltpu.roll(acc1_ref[...], nch - 1, 0)
    out = _dot(jax.nn.gelu(hid).astype(BF16), w2_ref[...])
    k_ref[...] = out[:, 0:LANES].astype(BF16).reshape(k_ref.shape)
    v_ref[...] = out[:, LANES:2 * LANES].astype(BF16).reshape(v_ref.shape)


def _compress_prompt_kernel(xk_ref, xv_ref, pe_ref, w1_ref, w2_ref, k_ref, v_ref, acc0_ref, acc1_ref):
    nch = xk_ref.shape[0] // CMP_STRIDE
    rows = lambda s: pl.ds(s, nch, stride=CMP_STRIDE)
    load = lambda s: jnp.concatenate([xk_ref[rows(s), :], xv_ref[rows(s), :]], axis=1)
    _compress_core(load, nch, pe_ref, w1_ref, w2_ref, acc0_ref, acc1_ref, k_ref, v_ref)


def _compress_prompt(kv, B, S, pe, w1, w2):
    nch = S // CMP_STRIDE
    out = jax.ShapeDtypeStruct((B, nch, LANES), BF16)
    return pl.pallas_call(
        _compress_prompt_kernel,
        grid=(B,),
        in_specs=[pl.BlockSpec((S, LANES), lambda b: (b, 0)), pl.BlockSpec((S, LANES), lambda b: (b, 1)),
                  _full(pe.shape), _full(w1.shape), _full(w2.shape)],
        out_specs=[pl.BlockSpec((1, nch, LANES), lambda b: (b, 0, 0))] * 2,
        out_shape=[out, out],
        scratch_shapes=[pltpu.VMEM((nch, 2 * LANES), F32)] * 2,
        compiler_params=_params(("parallel",)),
        name="compress_prompt",
    )(kv, kv, pe, w1, w2)


def _compress_sample_kernel(n_pages, pt_ref, *refs):
    pages = refs[:n_pages]
    pe_ref, w1_ref, w2_ref, k_ref, v_ref, acc0_ref, acc1_ref, xk_ref, xv_ref = refs[n_pages:]
    page = pages[0].shape[2]
    per = page // CMP_STRIDE
    nch = n_pages * per
    for j, p in enumerate(pages):
        xt = p[0].T
        for c in range(per):
            dst = slice((j * per + c) * CMP_PITCH, (j * per + c) * CMP_PITCH + CMP_STRIDE)
            src = slice(c * CMP_STRIDE, (c + 1) * CMP_STRIDE)
            xk_ref[dst, :] = xt[src, 0:LANES]
            xv_ref[dst, :] = xt[src, LANES:2 * LANES]
    rows = lambda s: pl.ds(s, nch, stride=CMP_PITCH)
    load = lambda s: jnp.concatenate([xk_ref[rows(s), :], xv_ref[rows(s), :]], axis=1)
    _compress_core(load, nch, pe_ref, w1_ref, w2_ref, acc0_ref, acc1_ref, k_ref, v_ref)


def _page_specs(n_pages, page, base, row_block):
    def spec(j):
        return pl.BlockSpec((1, 2 * LANES, page),
                            lambda n, pt: (base + pt[n * n_pages + j], row_block, 0))
    return [spec(j) for j in range(n_pages)]


def _compress_sample(cache_t, pt_flat, NB, n_pages, page, base, pe, w1, w2):
    nch = n_pages * page // CMP_STRIDE
    out = jax.ShapeDtypeStruct((NB * nch, LANES), BF16)
    cfull = lambda shape: pl.BlockSpec(shape, lambda n, pt: (0,) * len(shape))
    grid_spec = pltpu.PrefetchScalarGridSpec(
        num_scalar_prefetch=1,
        grid=(NB,),
        in_specs=_page_specs(n_pages, page, base, 0) + [cfull(pe.shape), cfull(w1.shape), cfull(w2.shape)],
        out_specs=[pl.BlockSpec((nch, LANES), lambda n, pt: (n, 0))] * 2,
        scratch_shapes=[pltpu.VMEM((nch, 2 * LANES), F32)] * 2 + [pltpu.VMEM((nch * CMP_PITCH, LANES), F32)] * 2,
    )
    return pl.pallas_call(
        functools.partial(_compress_sample_kernel, n_pages),
        grid_spec=grid_spec,
        out_shape=[out, out],
        compiler_params=_params(("parallel",)),
        name="compress_sample",
    )(pt_flat, *([cache_t] * n_pages), pe, w1, w2)


def _group_queries(q, kv):
    lane = lax.broadcasted_iota(jnp.int32, (q.shape[0], LANES), 1)
    mine = (lane < HEAD_DIM) if kv == 0 else (lane >= HEAD_DIM)
    parts = []
    for g in range(GQ):
        h = kv * GQ + g
        t = q[:, (h // 2) * LANES:(h // 2 + 1) * LANES]
        if h % 2 != kv:
            t = pltpu.roll(t, HEAD_DIM, 1)
        parts.append(jnp.where(mine, t, 0.0))
    return jnp.concatenate(parts, axis=0).astype(BF16)


def _softmax_parts(s, mask):
    s = jnp.where(mask, s, NEG_INF)
    m = jnp.maximum(jnp.max(s, axis=-1, keepdims=True), M_FLOOR)
    e = jnp.exp(s - m)
    return e.astype(BF16), 1.0 / jnp.maximum(jnp.sum(e, axis=-1, keepdims=True), TINY)


def _select_blocks(score):
    st = score.T
    blk = lax.broadcasted_iota(jnp.int32, st.shape, 0).astype(F32)

    def body(_, sc):
        m = jnp.max(sc, axis=0, keepdims=True)
        first = jnp.min(jnp.where(sc == m, blk, float(LANES)), axis=0, keepdims=True)
        return jnp.where(blk == first, PICKED, sc)

    picked = lax.fori_loop(0, N_SELECT, body, st)
    return jnp.where(picked == PICKED, 1.0, 0.0).T


def _block_scores(imp, tq, n_slc):
    blk = lax.broadcasted_iota(jnp.int32, imp.shape, 1)
    cur = tq // SLC_BLOCK
    forced = (blk == 0) | (blk == cur) | (blk == cur - 1)
    allowed = blk * SLC_BLOCK <= tq
    score = jnp.where(forced, FORCE_SCORE, jnp.where(allowed, imp, -1.0))
    return jnp.where(blk < n_slc, score, -3.0)


def _cover(ci, sj):
    return ((ci * CMP_STRIDE < (sj + 1) * SLC_BLOCK) & (ci * CMP_STRIDE + CMP_BLOCK > sj * SLC_BLOCK))


def _nsa_prompt_kernel(q_ref, gate_ref, ck_ref, cv_ref, kvb_ref, et_ref, out_ref, *, seq):
    i = pl.program_id(1)
    R = Q_BLOCK
    s0 = i * R
    q = q_ref[...]
    gate = gate_ref[...]
    ncb = ck_ref.shape[1]
    n_cmp = seq // CMP_STRIDE - 1
    n_slc = seq // SLC_BLOCK
    tok4 = lax.broadcasted_iota(jnp.int32, (GQ * R, 1), 0) % R
    tq4 = s0 + tok4
    tq = s0 + lax.broadcasted_iota(jnp.int32, (R, 1), 0)
    ck = ck_ref[0]
    cv = cv_ref[0]
    ci = lax.broadcasted_iota(jnp.int32, (ncb, LANES), 0)
    sj = lax.broadcasted_iota(jnp.int32, (ncb, LANES), 1)
    cover = jnp.where(_cover(ci, sj) & (ci < n_cmp) & (sj < n_slc), 1.0, 0.0).astype(BF16)
    cmp_i = lax.broadcasted_iota(jnp.int32, (GQ * R, ncb), 1)
    cmp_mask = (cmp_i * CMP_STRIDE + (CMP_BLOCK - 1) <= tq4) & (cmp_i < n_cmp)
    n_tiles = (s0 + R + KEY_TILE - 1) // KEY_TILE
    wstart = pl.multiple_of(jnp.maximum(s0 - WINDOW, 0), R)
    wlen = WINDOW + R
    wpos = wstart + lax.broadcasted_iota(jnp.int32, (GQ * R, wlen), 1)
    wdiff = tq4 - wpos
    wmask = (wdiff >= 0) & (wdiff < WINDOW)
    kcol = lax.broadcasted_iota(jnp.int32, (GQ * R, KEY_TILE), 1)

    qzs = [_group_queries(q, kv) for kv in range(N_KV)]
    kw = kvb_ref[pl.ds(wstart, wlen), 2 * LANES:3 * LANES]
    vw = kvb_ref[pl.ds(wstart, wlen), 3 * LANES:4 * LANES]
    s_cs = [_dot_nt(qz, ck) for qz in qzs]
    s_ws = [_dot_nt(qz, kw) for qz in qzs]
    o_cs, o_ws, scores = [], [], []
    for kv in range(N_KV):
        e_c, inv = _softmax_parts(s_cs[kv], cmp_mask)
        o_cs.append(_dot(e_c, cv) * inv)
        imp4 = _dot(e_c, cover) * inv
        imp = imp4[0:R] + imp4[R:2 * R] + imp4[2 * R:3 * R] + imp4[3 * R:4 * R]
        scores.append(_block_scores(imp, tq, n_slc))
    for kv in range(N_KV):
        e_w, inv = _softmax_parts(s_ws[kv], wmask)
        o_ws.append(_dot(e_w, vw) * inv)
    sel = _select_blocks(jnp.concatenate(scores, axis=0))

    q_augs = []
    for kv in range(N_KV):
        selneg = jnp.where(sel[kv * R:(kv + 1) * R] > 0.5, 0.0, NEG_INF).astype(BF16)
        q_augs.append(jnp.concatenate([qzs[kv], jnp.concatenate([selneg] * GQ, axis=0)], axis=1))

    cw = GQ * R // SEL_SPLIT
    chains = [(kv, c) for kv in range(N_KV) for c in range(SEL_SPLIT)]
    q_t = []
    for kv in range(N_KV):
        qt = q_augs[kv].astype(F32).T.astype(BF16)
        q_t += [qt[:, c * cw:(c + 1) * cw] for c in range(SEL_SPLIT)]
    krow = lax.broadcasted_iota(jnp.int32, (KEY_TILE, cw), 0)
    tq_rows = [s0 + (c * cw + lax.broadcasted_iota(jnp.int32, (1, cw), 1)) % R for _, c in chains]

    def tile_step(off, carry, diagonal):
        k_aug = jnp.concatenate([kvb_ref[pl.ds(off, KEY_TILE), 0:LANES],
                                 et_ref[pl.ds(off, KEY_TILE), :]], axis=1)
        vt = kvb_ref[pl.ds(off, KEY_TILE), LANES:2 * LANES]
        scores_t = [_dot(k_aug, qt) for qt in q_t]
        out = []
        for s, tq_row, (m, l, acc) in zip(scores_t, tq_rows, carry):
            if diagonal:
                s = jnp.where(off + krow <= tq_row, s, NEG_INF)
            m_new = jnp.maximum(m, jnp.max(s, axis=0, keepdims=True))
            alpha = jnp.exp(m - m_new)
            p = jnp.exp(s - m_new)
            l = alpha * l + jnp.sum(p, axis=0, keepdims=True)
            pv = lax.dot_general(vt, p.astype(BF16), (((0,), (0,)), ((), ())), preferred_element_type=F32)
            out.append((m_new, l, alpha * acc + pv))
        return tuple(out)

    init1 = (jnp.full((1, cw), NEG_INF, F32), jnp.zeros((1, cw), F32), jnp.zeros((LANES, cw), F32))
    carry = lax.fori_loop(
        0, n_tiles - 1, lambda t, c: tile_step(pl.multiple_of(t * KEY_TILE, KEY_TILE), c, False),
        (init1,) * len(chains))
    carry = tile_step(pl.multiple_of((n_tiles - 1) * KEY_TILE, KEY_TILE), carry, True)

    for kv in range(N_KV):
        qz, o_c = qzs[kv], o_cs[kv]
        parts = [acc / jnp.maximum(l, TINY) for _, l, acc in carry[kv * SEL_SPLIT:(kv + 1) * SEL_SPLIT]]
        o_s = jnp.concatenate(parts, axis=1).T

        o_w = o_ws[kv]

        for g in range(GQ):
            h = kv * GQ + g
            rows = slice(g * R, (g + 1) * R)
            lanes = slice(kv * HEAD_DIM, (kv + 1) * HEAD_DIM)
            o = (gate[:, 3 * h:3 * h + 1] * o_c[rows, lanes]
                 + gate[:, 3 * h + 1:3 * h + 2] * o_s[rows, lanes]
                 + gate[:, 3 * h + 2:3 * h + 3] * o_w[rows, lanes])
            out_ref[:, h * HEAD_DIM:(h + 1) * HEAD_DIM] = o


def _nsa_prompt(q, gate, ck, cv, kvb, emat, B, S):
    nq = S // Q_BLOCK
    ncb = ck.shape[1]
    rowblk = lambda w_: pl.BlockSpec((Q_BLOCK, w_), lambda b, i: (b * nq + i, 0))
    return pl.pallas_call(
        functools.partial(_nsa_prompt_kernel, seq=S),
        grid=(B, nq),
        in_specs=[rowblk(Q_W), rowblk(GATE_PAD),
                  pl.BlockSpec((1, ncb, LANES), lambda b, i: (b, 0, 0)),
                  pl.BlockSpec((1, ncb, LANES), lambda b, i: (b, 0, 0)),
                  pl.BlockSpec((S, 4 * LANES), lambda b, i: (b, 0)),
                  pl.BlockSpec(emat.shape, lambda b, i: (0, 0))],
        out_specs=rowblk(Q_W),
        out_shape=jax.ShapeDtypeStruct((B * S, Q_W), F32),
        compiler_params=_params(("parallel", "arbitrary")),
        name="nsa_prompt",
    )(q, gate, ck, cv, kvb, emat)


def _nsa_sample_select_kernel(q_ref, gate_ref, ck_ref, cv_ref, sel_ref, oc_ref, *, past_len, tn, ncb):
    R = q_ref.shape[0]
    q = q_ref[...]
    gate = gate_ref[...]
    total = past_len + tn
    n_cmp = total // CMP_STRIDE - 1
    n_slc = -(-total // SLC_BLOCK)
    ncol = ck_ref.shape[0]
    row4 = lax.broadcasted_iota(jnp.int32, (GQ * R, 1), 0) % R
    tq4 = past_len + row4 % tn
    tq = past_len + lax.broadcasted_iota(jnp.int32, (R, 1), 0) % tn
    col = lax.broadcasted_iota(jnp.int32, (GQ * R, ncol), 1)
    ci = col % ncb
    cmp_mask = ((col // ncb == row4 // tn) & (ci * CMP_STRIDE + (CMP_BLOCK - 1) <= tq4) & (ci < n_cmp))
    cr = lax.broadcasted_iota(jnp.int32, (ncol, LANES), 0) % ncb
    sj = lax.broadcasted_iota(jnp.int32, (ncol, LANES), 1)
    cover = jnp.where(_cover(cr, sj) & (cr < n_cmp) & (sj < n_slc), 1.0, 0.0).astype(BF16)
    ck = ck_ref[...]
    cv = cv_ref[...]
    scores = []
    for kv in range(N_KV):
        qz = _group_queries(q, kv)
        e_c, inv = _softmax_parts(_dot_nt(qz, ck), cmp_mask)
        o_c = _dot(e_c, cv) * inv
        imp4 = _dot(e_c, cover) * inv
        imp = imp4[0:R] + imp4[R:2 * R] + imp4[2 * R:3 * R] + imp4[3 * R:4 * R]
        scores.append(_block_scores(imp, tq, n_slc))
        for g in range(GQ):
            h = kv * GQ + g
            oc_ref[:, h * HEAD_DIM:(h + 1) * HEAD_DIM] = (
                gate[:, 3 * h:3 * h + 1] * o_c[g * R:(g + 1) * R, kv * HEAD_DIM:(kv + 1) * HEAD_DIM])
    sel = _select_blocks(jnp.concatenate(scores, axis=0))
    for kv in range(N_KV):
        sel_ref[:, kv * LANES:(kv + 1) * LANES] = jnp.where(sel[kv * R:(kv + 1) * R] > 0.5, 0.0, NEG_INF)


def _nsa_sample_select(q, gate, ck, cv, NB, tn, past_len, ncb):
    R = NSA_SEQ_GROUP * tn
    rowblk = lambda w_: pl.BlockSpec((R, w_), lambda i: (i, 0))
    cblk = pl.BlockSpec((NSA_SEQ_GROUP * ncb, LANES), lambda i: (i, 0))
    return pl.pallas_call(
        functools.partial(_nsa_sample_select_kernel, past_len=past_len, tn=tn, ncb=ncb),
        grid=(NB // NSA_SEQ_GROUP,),
        in_specs=[rowblk(Q_W), rowblk(GATE_PAD), cblk, cblk],
        out_specs=[rowblk(2 * LANES), rowblk(Q_W)],
        out_shape=[jax.ShapeDtypeStruct((NB * tn, 2 * LANES), F32),
                   jax.ShapeDtypeStruct((NB * tn, Q_W), F32)],
        compiler_params=_params(("parallel",)),
        name="nsa_sample_select",
    )(q, gate, ck, cv)


def _nsa_sample_attend_kernel(n_pages, pt_ref, *refs, past_len, tn):
    pages = refs[:n_pages]
    q_ref, gate_ref, sel_ref, oc_ref, kvn_ref, win_ref, e_ref, out_ref = refs[n_pages:]
    R = tn
    q = q_ref[...]
    gate = gate_ref[...]
    kvn = kvn_ref[...]
    wb = win_ref.shape[2]
    new_blk = past_len // SLC_BLOCK
    rows_all = N_KV * GQ * R
    tok4 = lax.broadcasted_iota(jnp.int32, (rows_all, 1), 0) % R
    newcol = lax.broadcasted_iota(jnp.int32, (rows_all, LANES), 1)
    causal_new = newcol <= tok4
    kt_past = jnp.concatenate([p[0, 0:LANES, :] for p in pages], axis=1).astype(BF16)
    vt_past = jnp.concatenate([p[0, LANES:2 * LANES, :] for p in pages], axis=1).astype(BF16)
    pad = jnp.zeros((LANES - tn, KV_W), F32)
    kvn = jnp.concatenate([kvn, pad], axis=0)
    k_new = kvn[:, 2 * LANES:3 * LANES].astype(BF16)
    v_new = kvn[:, 3 * LANES:4 * LANES].astype(BF16)
    kwt = win_ref[0, 0:LANES, :].astype(BF16)
    vwt = win_ref[0, LANES:2 * LANES, :].astype(BF16)
    kw_new = kvn[:, 4 * LANES:5 * LANES].astype(BF16)
    vw_new = kvn[:, 5 * LANES:6 * LANES].astype(BF16)
    wcol = lax.broadcasted_iota(jnp.int32, (rows_all, wb), 1)
    wdiff = wb + tok4 - wcol
    wmask = (wdiff >= 0) & (wdiff < WINDOW) & (wcol >= wb - past_len)

    qz = jnp.concatenate([_group_queries(q, kv) for kv in range(N_KV)], axis=0)
    sel4 = jnp.concatenate([sel_ref[:, kv * LANES:(kv + 1) * LANES] for kv in range(N_KV) for _ in range(GQ)],
                           axis=0)
    s = _dot(qz, kt_past) + _dot(sel4.astype(BF16), e_ref[...])
    s_new = jnp.where(causal_new, _dot_nt(qz, k_new) + sel4[:, new_blk:new_blk + 1], NEG_INF)
    s_w = jnp.where(wmask, _dot(qz, kwt), NEG_INF)
    s_wn = jnp.where(causal_new, _dot_nt(qz, kw_new), NEG_INF)

    m = jnp.maximum(jnp.max(s, axis=-1, keepdims=True), jnp.max(s_new, axis=-1, keepdims=True))
    e = jnp.exp(s - m)
    e_new = jnp.exp(s_new - m)
    l = jnp.sum(e, axis=-1, keepdims=True) + jnp.sum(e_new, axis=-1, keepdims=True)
    o_s_all = (_dot_nt(e.astype(BF16), vt_past) + _dot(e_new.astype(BF16), v_new)) / jnp.maximum(l, TINY)

    m = jnp.maximum(jnp.max(s_w, axis=-1, keepdims=True), jnp.max(s_wn, axis=-1, keepdims=True))
    e = jnp.exp(s_w - m)
    e_new = jnp.exp(s_wn - m)
    l = jnp.sum(e, axis=-1, keepdims=True) + jnp.sum(e_new, axis=-1, keepdims=True)
    o_w_all = (_dot_nt(e.astype(BF16), vwt) + _dot(e_new.astype(BF16), vw_new)) / jnp.maximum(l, TINY)

    for kv in range(N_KV):
        o_s = o_s_all[kv * GQ * R:(kv + 1) * GQ * R]
        o_w = o_w_all[kv * GQ * R:(kv + 1) * GQ * R]
        for g in range(GQ):
            h = kv * GQ + g
            rows = slice(g * R, (g + 1) * R)
            lanes = slice(kv * HEAD_DIM, (kv + 1) * HEAD_DIM)
            hl = slice(h * HEAD_DIM, (h + 1) * HEAD_DIM)
            out_ref[:, hl] = (oc_ref[:, hl] + gate[:, 3 * h + 1:3 * h + 2] * o_s[rows, lanes]
                              + gate[:, 3 * h + 2:3 * h + 3] * o_w[rows, lanes])


def _nsa_sample_attend(cache, win, pt_flat, q, gate, selneg, ocg, kvn, emat,
                       NB, tn, n_pages, page, base, win_base):
    past_len = n_pages * page
    wb = win.shape[2]
    rowblk = lambda w_: pl.BlockSpec((tn, w_), lambda n, pt: (n, 0))
    grid_spec = pltpu.PrefetchScalarGridSpec(
        num_scalar_prefetch=1,
        grid=(NB,),
        in_specs=_page_specs(n_pages, page, base, 1) + [
            rowblk(Q_W), rowblk(GATE_PAD), rowblk(2 * LANES), rowblk(Q_W), rowblk(KV_W),
            pl.BlockSpec((1, 2 * LANES, wb), lambda n, pt: (win_base + n, 0, 0)),
            pl.BlockSpec(emat.shape, lambda n, pt: (0, 0))],
        out_specs=rowblk(Q_W),
    )
    return pl.pallas_call(
        functools.partial(_nsa_sample_attend_kernel, n_pages, past_len=past_len, tn=tn),
        grid_spec=grid_spec,
        out_shape=jax.ShapeDtypeStruct((NB * tn, Q_W), F32),
        compiler_params=_params(("parallel",)),
        name="nsa_sample_attend",
    )(pt_flat, *([cache] * n_pages), q, gate, selneg, ocg, kvn, win, emat)


def _ffn(x1, gf_ref, w1_ref, w3_ref, w2_ref, hb_ref, acc_ref):
    hb_ref[...] = _rms(x1, gf_ref[...]).astype(BF16)
    acc_ref[...] = x1

    def body(c, carry):
        off = pl.multiple_of(c * FF_CHUNK, FF_CHUNK)
        h = hb_ref[...]
        a = _dot(h, w1_ref[:, pl.ds(off, FF_CHUNK)])
        b = _dot(h, w3_ref[:, pl.ds(off, FF_CHUNK)])
        gact = (a * _sigmoid(a) * b).astype(BF16)
        acc_ref[...] += _dot(gact, w2_ref[pl.ds(off, FF_CHUNK), :])
        return carry

    lax.fori_loop(0, D_FF // FF_CHUNK, body, 0)
    return acc_ref[...]


def _finish(x2, final, gfin_ref, out_ref):
    out_ref[...] = _rms(x2, gfin_ref[...]) if final else x2


def _even_tail_kernel(x_ref, attn_ref, u_ref, vn_ref, wmix_ref, bmix_ref, wo_ref,
                      gf_ref, w1_ref, w3_ref, w2_ref, gfin_ref, out_ref,
                      sgu_ref, hb_ref, acc_ref, *, chunk, final):
    tm = x_ref.shape[0]
    r = lax.broadcasted_iota(jnp.int32, (SGU_CHUNK, SGU_CHUNK), 0)
    c = lax.broadcasted_iota(jnp.int32, (SGU_CHUNK, SGU_CHUNK), 1)
    causal = (c <= r) & (r // chunk == c // chunk)
    for g in range(SGU_GROUPS):
        w = jnp.where(causal, wmix_ref[g], 0.0).astype(BF16)
        b = bmix_ref[g]
        lanes = slice(g * SGU_DIM, (g + 1) * SGU_DIM)
        for k in range(tm // SGU_CHUNK):
            rows = slice(k * SGU_CHUNK, (k + 1) * SGU_CHUNK)
            mix = _dot(w, vn_ref[rows, lanes].astype(BF16)) + b
            sgu_ref[rows, lanes] = (u_ref[rows, lanes] * mix).astype(BF16)
    x1 = (x_ref[...] + _dot(attn_ref[...].astype(BF16), wo_ref[0:Q_W, :])
          + _dot(sgu_ref[...], wo_ref[Q_W:Q_W + U_W, :]))
    _finish(_ffn(x1, gf_ref, w1_ref, w3_ref, w2_ref, hb_ref, acc_ref), final, gfin_ref, out_ref)


def _odd_tail_kernel(x_ref, z_ref, wa_ref, wb_ref, gf_ref, w1_ref, w3_ref, w2_ref, gfin_ref, out_ref,
                     hb_ref, acc_ref, *, final):
    z = z_ref[...]
    x1 = x_ref[...] + _dot(z, wa_ref[...]) * _sigmoid(_dot(z, wb_ref[...]))
    _finish(_ffn(x1, gf_ref, w1_ref, w3_ref, w2_ref, hb_ref, acc_ref), final, gfin_ref, out_ref)


def _ffn_specs():
    return [_full((1, D_MODEL)), _full((D_MODEL, D_FF)), _full((D_MODEL, D_FF)), _full((D_FF, D_MODEL)),
            _full((1, D_MODEL))]


def _even_tail(x, attn, u, vn, wmix, bmix, wo, gf, w1, w3, w2, gfin, chunk, final):
    T = x.shape[0]
    tm = min(TOKEN_TILE, T)
    row = lambda w_: pl.BlockSpec((tm, w_), lambda i: (i, 0))
    return pl.pallas_call(
        functools.partial(_even_tail_kernel, chunk=chunk, final=final),
        grid=(T // tm,),
        in_specs=[row(D_MODEL), row(Q_W), row(U_W), row(U_W), _full(wmix.shape), _full(bmix.shape),
                  _full(wo.shape)] + _ffn_specs(),
        out_specs=row(D_MODEL),
        out_shape=jax.ShapeDtypeStruct((T, D_MODEL), F32),
        scratch_shapes=[pltpu.VMEM((tm, U_W), BF16), pltpu.VMEM((tm, D_MODEL), BF16),
                        pltpu.VMEM((tm, D_MODEL), F32)],
        compiler_params=_params(("parallel",)),
        name="even_tail",
    )(x, attn, u, vn, wmix, bmix, wo, gf, w1, w3, w2, gfin)


def _odd_tail(x, z, wa, wb, gf, w1, w3, w2, gfin, final):
    T = x.shape[0]
    tm = min(TOKEN_TILE, T)
    row = lambda w_: pl.BlockSpec((tm, w_), lambda i: (i, 0))
    return pl.pallas_call(
        functools.partial(_odd_tail_kernel, final=final),
        grid=(T // tm,),
        in_specs=[row(D_MODEL), row(D_MODEL), _full(wa.shape), _full(wb.shape)] + _ffn_specs(),
        out_specs=row(D_MODEL),
        out_shape=jax.ShapeDtypeStruct((T, D_MODEL), F32),
        scratch_shapes=[pltpu.VMEM((tm, D_MODEL), BF16), pltpu.VMEM((tm, D_MODEL), F32)],
        compiler_params=_params(("parallel",)),
        name="odd_tail",
    )(x, z, wa, wb, gf, w1, w3, w2, gfin)


def _s5_disc_kernel(are_ref, aim_ref, ls_ref, bre_ref, bim_ref, abre_ref, abim_ref, bbre_ref, bbim_ref):
    a_re = are_ref[...]
    a_im = aim_ref[...]
    dt = jnp.exp(ls_ref[...])
    lr = a_re * dt
    li = a_im * dt
    mag = jnp.exp(lr)
    ab_re = mag * jnp.cos(li)
    ab_im = mag * jnp.sin(li)
    den = a_re * a_re + a_im * a_im
    nr = ab_re - 1.0
    cr = (nr * a_re + ab_im * a_im) / den
    cim = (ab_im * a_re - nr * a_im) / den
    b_re = bre_ref[...]
    b_im = bim_ref[...]
    abre_ref[...] = ab_re
    abim_ref[...] = ab_im
    bbre_ref[...] = cr * b_re - cim * b_im
    bbim_ref[...] = cr * b_im + cim * b_re


def _s5_disc(a_re, a_im, log_step, b_re, b_im):
    col = jax.ShapeDtypeStruct((S5_W, 1), F32)
    mat = jax.ShapeDtypeStruct((S5_W, S5_GROUP), F32)
    ls = jnp.broadcast_to(log_step[:, None], (S5_GROUPS, S5_STATE)).reshape(S5_W, 1)
    return pl.pallas_call(
        _s5_disc_kernel,
        out_shape=[col, col, mat, mat],
        name="s5_disc",
    )(a_re.reshape(S5_W, 1), a_im.reshape(S5_W, 1), ls,
      b_re.reshape(S5_W, S5_GROUP), b_im.reshape(S5_W, S5_GROUP))


S5_BLOCKS = 4
S5_BLK_CH = D_MODEL // S5_BLOCKS
S5_BLK_ST = S5_W // S5_BLOCKS


def _s5_input_states(ub, bw_ref, bu_ref):
    for k in range(S5_BLOCKS):
        r = _dot(ub[:, k * S5_BLK_CH:(k + 1) * S5_BLK_CH], bw_ref[k])
        bu_ref[:, k * S5_BLK_ST:(k + 1) * S5_BLK_ST] = r[:, 0:S5_BLK_ST]
        bu_ref[:, S5_W + k * S5_BLK_ST:S5_W + (k + 1) * S5_BLK_ST] = r[:, S5_BLK_ST:2 * S5_BLK_ST]


def _s5_output(u, h_ref, cre_ref, cim_ref, d_ref, store):
    for k in range(S5_BLOCKS):
        hr = h_ref[:, k * S5_BLK_ST:(k + 1) * S5_BLK_ST].astype(BF16)
        hi = h_ref[:, S5_W + k * S5_BLK_ST:S5_W + (k + 1) * S5_BLK_ST].astype(BF16)
        cols = slice(k * S5_BLK_CH, (k + 1) * S5_BLK_CH)
        y = _dot(hr, cre_ref[k]) - _dot(hi, cim_ref[k]) + d_ref[:, cols] * u[:, cols]
        store(cols, jax.nn.gelu(y).astype(BF16))


def _s5_prompt_kernel(x_ref, g_ref, ab_ref, bw_ref, cre_ref, cim_ref, d_ref,
                      z_ref, hre_ref, him_ref, bu_ref, st_ref, pw_ref, up_ref):
    c = pl.program_id(1)
    tc = x_ref.shape[0]
    seg = tc // S5_SEGMENTS
    ntile = S5_W // LANES

    @pl.when(c == 0)
    def _():
        st_ref[...] = jnp.zeros_like(st_ref)
        ar = ab_ref[0:1, :]
        ai = ab_ref[1:2, :]

        def put(t, pr, pi):
            for j in range(ntile):
                pw_ref[j, pl.ds(t, 1), :] = pr[:, j * LANES:(j + 1) * LANES]
                pw_ref[ntile + j, pl.ds(t, 1), :] = pi[:, j * LANES:(j + 1) * LANES]

        def power(t, carry):
            pr, pi = carry
            nr = ar * pr - ai * pi
            ni = ar * pi + ai * pr
            put(t, nr, ni)
            return nr, ni

        put(0, ar, ai)
        lax.fori_loop(1, seg, power, (ar, ai))

    pitch = seg + S5_SEG_PAD
    u = _rms(x_ref[...], g_ref[...])
    up_ref[...] = jnp.zeros_like(up_ref)
    for s in range(S5_SEGMENTS):
        up_ref[s * pitch:s * pitch + seg, :] = u[s * seg:(s + 1) * seg, :]
    up = up_ref[...]
    ub = up.astype(BF16)
    for k in range(S5_BLOCKS):
        r = _dot(ub[:, k * S5_BLK_CH:(k + 1) * S5_BLK_CH], bw_ref[k])
        for j in range(S5_BLK_ST // LANES):
            t = k * (S5_BLK_ST // LANES) + j
            bu_ref[t] = r[:, j * LANES:(j + 1) * LANES]
            bu_ref[ntile + t] = r[:, S5_BLK_ST + j * LANES:S5_BLK_ST + (j + 1) * LANES]

    for t0 in range(0, ntile, S5_TILE_GROUP):
        tiles = range(t0, t0 + S5_TILE_GROUP)
        rep = lambda v: jnp.broadcast_to(v, (S5_SEGMENTS, LANES))
        a = [(rep(ab_ref[0:1, t * LANES:(t + 1) * LANES]), rep(ab_ref[1:2, t * LANES:(t + 1) * LANES]))
             for t in tiles]

        def local(s, hc):
            rows = pl.ds(s, S5_SEGMENTS, stride=pitch)
            out = []
            for (ar, ai), (hr, hi), t in zip(a, hc, tiles):
                nr = ar * hr - ai * hi + bu_ref[t, rows, :]
                ni = ar * hi + ai * hr + bu_ref[ntile + t, rows, :]
                bu_ref[t, rows, :] = nr
                bu_ref[ntile + t, rows, :] = ni
                out.append((nr, ni))
            return tuple(out)

        zero = jnp.zeros((S5_SEGMENTS, LANES), F32)
        ends = lax.fori_loop(0, seg, local, ((zero, zero),) * S5_TILE_GROUP, unroll=4)
        h0 = []
        for (er, ei), t in zip(ends, tiles):
            lanes = slice(t * LANES, (t + 1) * LANES)
            qr = pw_ref[t, seg - 1:seg, :]
            qi = pw_ref[ntile + t, seg - 1:seg, :]
            hr = st_ref[0:1, lanes]
            hi = st_ref[1:2, lanes]
            hrs, his = [], []
            for s in range(S5_SEGMENTS):
                hrs.append(hr)
                his.append(hi)
                hr, hi = (er[s:s + 1] + qr * hr - qi * hi, ei[s:s + 1] + qr * hi + qi * hr)
            st_ref[0:1, lanes] = hr
            st_ref[1:2, lanes] = hi
            h0.append((jnp.concatenate(hrs, axis=0), jnp.concatenate(his, axis=0)))

        def fix(s, carry_):
            rows = pl.ds(s, S5_SEGMENTS, stride=pitch)
            for (h0r, h0i), t in zip(h0, tiles):
                pr = pw_ref[t, pl.ds(s, 1), :]
                pi = pw_ref[ntile + t, pl.ds(s, 1), :]
                bu_ref[t, rows, :] = bu_ref[t, rows, :] + (pr * h0r - pi * h0i)
                bu_ref[ntile + t, rows, :] = bu_ref[ntile + t, rows, :] + (pr * h0i + pi * h0r)
            return carry_

        lax.fori_loop(0, seg, fix, 0, unroll=2)

    per = S5_BLK_ST // LANES
    for k in range(S5_BLOCKS):
        hr = jnp.concatenate([bu_ref[k * per + j] for j in range(per)], axis=1).astype(BF16)
        hi = jnp.concatenate([bu_ref[ntile + k * per + j] for j in range(per)], axis=1).astype(BF16)
        cols = slice(k * S5_BLK_CH, (k + 1) * S5_BLK_CH)
        y = _dot(hr, cre_ref[k]) - _dot(hi, cim_ref[k]) + d_ref[:, cols] * up[:, cols]
        zp = jax.nn.gelu(y)
        for s in range(S5_SEGMENTS):
            z_ref[s * seg:(s + 1) * seg, cols] = zp[s * pitch:s * pitch + seg, :].astype(BF16)
    hre_ref[0] = st_ref[0:1, :]
    him_ref[0] = st_ref[1:2, :]


def _s5_prompt(x, g, ab, bw, cre, cim, d, B, S):
    tc = min(S5_CHUNK_ROWS, S)
    nc = S // tc
    padded = tc + S5_SEGMENTS * S5_SEG_PAD
    st = jax.ShapeDtypeStruct((B, 1, S5_W), F32)
    return pl.pallas_call(
        _s5_prompt_kernel,
        grid=(B, nc),
        in_specs=[pl.BlockSpec((tc, D_MODEL), lambda b, c: (b * nc + c, 0)), _full((1, D_MODEL)),
                  _full(ab.shape), _full(bw.shape), _full(cre.shape), _full(cim.shape), _full((1, D_MODEL))],
        out_specs=[pl.BlockSpec((tc, D_MODEL), lambda b, c: (b * nc + c, 0)),
                   pl.BlockSpec((1, 1, S5_W), lambda b, c: (b, 0, 0)),
                   pl.BlockSpec((1, 1, S5_W), lambda b, c: (b, 0, 0))],
        out_shape=[jax.ShapeDtypeStruct((B * S, D_MODEL), BF16), st, st],
        scratch_shapes=[pltpu.VMEM((2 * S5_W // LANES, padded, LANES), F32), pltpu.VMEM((2, S5_W), F32),
                        pltpu.VMEM((2 * S5_W // LANES, tc // S5_SEGMENTS, LANES), F32),
                        pltpu.VMEM((padded, D_MODEL), F32)],
        compiler_params=_params(("parallel", "arbitrary")),
        name="s5_prompt",
    )(x, g, ab, bw, cre, cim, d)


def _s5_sample_kernel(x_ref, g_ref, ab_ref, bw_ref, cre_ref, cim_ref, d_ref, h0re_ref, h0im_ref,
                      z_ref, hre_ref, him_ref, bu_ref):
    tn, ns, _ = x_ref.shape
    u = _rms(x_ref[...].reshape(tn * ns, D_MODEL), g_ref[...])
    _s5_input_states(u.astype(BF16), bw_ref, bu_ref)
    ar = ab_ref[0:1, 0:S5_W]
    ai = ab_ref[1:2, 0:S5_W]
    hre_ref[...] = h0re_ref[...]
    him_ref[...] = h0im_ref[...]
    for j in range(tn):
        rows = slice(j * ns, (j + 1) * ns)
        hr = hre_ref[...]
        hi = him_ref[...]
        nr = ar * hr - ai * hi + bu_ref[rows, 0:S5_W]
        ni = ar * hi + ai * hr + bu_ref[rows, S5_W:2 * S5_W]
        hre_ref[...] = nr
        him_ref[...] = ni
        bu_ref[rows, 0:S5_W] = nr
        bu_ref[rows, S5_W:2 * S5_W] = ni

    def store(cols, val):
        z_ref[:, :, cols] = val.reshape(tn, ns, S5_BLK_CH)

    _s5_output(u, bu_ref, cre_ref, cim_ref, d_ref, store)


def _s5_sample(x_t, g, ab, bw, cre, cim, d, h0re, h0im):
    tn, NB, _ = x_t.shape
    ns = min(S5_SEQ_GROUP, NB)
    st = jax.ShapeDtypeStruct((NB, S5_W), F32)
    stb = pl.BlockSpec((ns, S5_W), lambda i: (i, 0))
    xb = pl.BlockSpec((tn, ns, D_MODEL), lambda i: (0, i, 0))
    return pl.pallas_call(
        _s5_sample_kernel,
        grid=(NB // ns,),
        in_specs=[xb, _full((1, D_MODEL)),
                  _full(ab.shape), _full(bw.shape), _full(cre.shape), _full(cim.shape), _full((1, D_MODEL)),
                  stb, stb],
        out_specs=[xb, stb, stb],
        out_shape=[jax.ShapeDtypeStruct((tn, NB, D_MODEL), BF16), st, st],
        scratch_shapes=[pltpu.VMEM((tn * ns, 2 * S5_W), F32)],
        compiler_params=_params(("parallel",)),
        name="s5_sample",
    )(x_t, g, ab, bw, cre, cim, d, h0re, h0im)


def _rope_tables(pos):
    half = ROPE_DIM // 2
    inv_freq = ROPE_THETA ** (-jnp.arange(half, dtype=F32) * (2.0 / ROPE_DIM))
    ang = pos.astype(F32)[:, None] * inv_freq[None, :]
    c, s = jnp.cos(ang), jnp.sin(ang)
    n = pos.shape[0]
    pad = HEAD_DIM - ROPE_DIM
    c64 = jnp.concatenate([c, c, jnp.ones((n, pad), F32)], axis=-1)
    s64 = jnp.concatenate([-s, s, jnp.zeros((n, pad), F32)], axis=-1)
    return jnp.tile(c64, (1, LANES // HEAD_DIM)), jnp.tile(s64, (1, LANES // HEAD_DIM))


def _w_in_layout(w):
    o = Q_W + KV_W
    gate = jnp.pad(w[:, o:o + GATE_W], ((0, 0), (0, GATE_PAD - GATE_W)))
    return jnp.concatenate([w[:, :o], w[:, o + GATE_W:], gate], axis=1).astype(BF16)


def _compress_layout(pe, w1, w2):
    w1r = w1.reshape(2, 2 * CMP_STRIDE, HEAD_DIM, HEAD_DIM)
    w1bd = jnp.zeros((2 * CMP_STRIDE, 2 * LANES, 2 * LANES), BF16)
    w2bd = jnp.zeros((2 * LANES, 2 * LANES), BF16)
    for c in range(2):
        for k in range(N_KV):
            o = (c * N_KV + k) * HEAD_DIM
            w1bd = w1bd.at[:, o:o + HEAD_DIM, o:o + HEAD_DIM].set(w1r[c].astype(BF16))
            w2bd = w2bd.at[o:o + HEAD_DIM, o:o + HEAD_DIM].set(w2[c].astype(BF16))
    per = pe.reshape(2, 2, CMP_STRIDE, HEAD_DIM)
    pel = jnp.broadcast_to(jnp.transpose(per, (1, 2, 0, 3))[:, :, :, None, :],
                           (2, CMP_STRIDE, 2, N_KV, HEAD_DIM)).reshape(2 * CMP_STRIDE, 2 * LANES)
    return pel, w1bd, w2bd


def _s5_layout(bb_re, bb_im, c_re, c_im):
    eye = jnp.eye(S5_GROUPS // S5_BLOCKS, dtype=F32)
    gl = S5_GROUPS // S5_BLOCKS
    bb = jnp.stack([bb_re, bb_im]).reshape(2, S5_BLOCKS, gl, S5_STATE, S5_GROUP)
    bw = jnp.einsum('rkgpc,gh->kgcrhp', bb, eye).reshape(S5_BLOCKS, S5_BLK_CH, 2 * S5_BLK_ST).astype(BF16)

    def cl(c):
        c = c.reshape(S5_BLOCKS, gl, S5_GROUP, S5_STATE)
        return jnp.einsum('kgcp,gh->kgphc', c, eye).reshape(S5_BLOCKS, S5_BLK_ST, S5_BLK_CH).astype(BF16)

    return bw, cl(c_re), cl(c_im)


def _block_onehot(n_keys):
    blk = jnp.arange(LANES)[:, None]
    key = jnp.arange(n_keys)[None, :]
    return (key // SLC_BLOCK == blk).astype(BF16)


def kernel(x_prompt, x_sample, cache_nsa_kv, cache_nsa_win, state_s5_re, state_s5_im, page_table, norm_mix, norm_ffn, norm_final, w_in, w_out, cmp_pe, cmp_w1, cmp_w2, sgu_ln_g, sgu_ln_b, sgu_w, sgu_b, s5_a_re, s5_a_im, s5_log_step, s5_b_re, s5_b_im, s5_c_re, s5_c_im, s5_d, glu_w_a, glu_w_b, ffn_w1, ffn_w3, ffn_w2):
    B, S, _ = x_prompt.shape
    NB, tn, _ = x_sample.shape
    n_even, pool, page = cache_nsa_kv.shape[:3]
    n_pages = page_table.shape[1]
    past_len = n_pages * page
    wb = cache_nsa_win.shape[2]
    assert S % KEY_TILE == 0 and S // SLC_BLOCK <= LANES and S >= WINDOW + Q_BLOCK
    assert past_len % SLC_BLOCK == 0 and tn <= CMP_STRIDE and SGU_CHUNK % tn == 0
    assert NB % NSA_SEQ_GROUP == 0 and -(-(past_len + tn) // SLC_BLOCK) <= LANES

    xp = x_prompt.reshape(B * S, D_MODEL)
    xs = x_sample.reshape(NB * tn, D_MODEL)
    cache = jnp.transpose(cache_nsa_kv, (0, 1, 3, 4, 5, 2)).reshape(n_even * pool, 4 * LANES, page)
    win = jnp.transpose(cache_nsa_win, (0, 1, 3, 4, 5, 2)).reshape(n_even * NB, 2 * LANES, wb)
    pt_flat = page_table.reshape(-1).astype(jnp.int32)
    cos_p, sin_p = _rope_tables(jnp.arange(S))
    cos_s, sin_s = _rope_tables(jnp.tile(past_len + jnp.arange(tn), NB))
    e_prompt = _block_onehot(S).T
    e_sample = _block_onehot(past_len)
    ncb_s = past_len // CMP_STRIDE
    tm_p = min(TOKEN_TILE, S)
    tm_s = min(TOKEN_TILE, NB * tn)
    row = lambda a: a.reshape(1, -1)

    kv_p, kv_s, win_p, win_s, sgu_v_s = [], [], [], [], []
    s5p_re, s5p_im, s5s_re, s5s_im = [], [], [], []
    for layer in range(DEPTH):
        final = layer == DEPTH - 1
        ffn = (row(norm_ffn[layer]), ffn_w1[layer].astype(BF16), ffn_w3[layer].astype(BF16),
               ffn_w2[layer].astype(BF16), row(norm_final))
        gmix = row(norm_mix[layer])
        if layer % 2 == 0:
            e = layer // 2
            wi = _w_in_layout(w_in[e])
            lng, lnb = row(sgu_ln_g[e]), row(sgu_ln_b[e])
            pel, w1bd, w2bd = _compress_layout(cmp_pe[e], cmp_w1[e], cmp_w2[e])
            qp, kvp, kvbp, gp, up, vnp = _inproj(xp, gmix, wi, cos_p, sin_p, lng, lnb, tm_p)
            qs, kvs, _, gs, us, vns = _inproj(xs, gmix, wi, cos_s, sin_s, lng, lnb, tm_s)
            ckp, cvp = _compress_prompt(kvp, B, S, pel, w1bd, w2bd)
            ap = _nsa_prompt(qp, gp, ckp, cvp, kvbp, e_prompt, B, S)
            cks, cvs = _compress_sample(cache, pt_flat, NB, n_pages, page, e * pool, pel, w1bd, w2bd)
            selneg, ocg = _nsa_sample_select(qs, gs, cks, cvs, NB, tn, past_len, ncb_s)
            a_s = _nsa_sample_attend(cache, win, pt_flat, qs, gs, selneg, ocg, kvs, e_sample,
                                     NB, tn, n_pages, page, e * pool, e * NB)
            wo = w_out[e].astype(BF16)
            bmix_p = sgu_b[e][:, :, None]
            reps = SGU_CHUNK // tn
            wmix_s = jnp.tile(sgu_w[e][:, :tn, :tn], (1, reps, reps))
            bmix_s = jnp.tile(sgu_b[e][:, :tn], (1, reps))[:, :, None]
            xp = _even_tail(xp, ap, up, vnp, sgu_w[e], bmix_p, wo, *ffn, chunk=SGU_CHUNK, final=final)
            xs = _even_tail(xs, a_s, us, vns, wmix_s, bmix_s, wo, *ffn, chunk=tn, final=final)
            kvp5 = kvp.reshape(B, S, 6, N_KV, HEAD_DIM)
            kvs5 = kvs.reshape(NB, tn, 6, N_KV, HEAD_DIM)
            kv_p.append(kvp5[:, :, 0:4])
            kv_s.append(kvs5[:, :, 0:4])
            win_p.append(kvp5[:, S - min(WINDOW, S):, 4:6])
            win_s.append(kvs5[:, :, 4:6])
            sgu_v_s.append(vns.reshape(NB, tn, SGU_GROUPS, SGU_DIM))
        else:
            o = layer // 2
            ab_re, ab_im, bb_re, bb_im = _s5_disc(s5_a_re[o], s5_a_im[o], s5_log_step[o], s5_b_re[o], s5_b_im[o])
            ab = jnp.concatenate([ab_re.reshape(1, S5_W), ab_im.reshape(1, S5_W)], axis=0)
            bw, cre, cim = _s5_layout(bb_re.reshape(S5_GROUPS, S5_STATE, S5_GROUP),
                                      bb_im.reshape(S5_GROUPS, S5_STATE, S5_GROUP), s5_c_re[o], s5_c_im[o])
            d = row(s5_d[o])
            zp, hpr, hpi = _s5_prompt(xp, gmix, ab, bw, cre, cim, d, B, S)
            xs_t = jnp.transpose(xs.reshape(NB, tn, D_MODEL), (1, 0, 2))
            zs_t, hsr, hsi = _s5_sample(xs_t, gmix, ab, bw, cre, cim, d,
                                        state_s5_re[o].reshape(NB, S5_W), state_s5_im[o].reshape(NB, S5_W))
            zs = jnp.transpose(zs_t, (1, 0, 2)).reshape(NB * tn, D_MODEL)
            wa, wb_ = glu_w_a[o].astype(BF16), glu_w_b[o].astype(BF16)
            xp = _odd_tail(xp, zp, wa, wb_, *ffn, final=final)
            xs = _odd_tail(xs, zs, wa, wb_, *ffn, final=final)
            s5p_re.append(hpr.reshape(B, S5_GROUPS, S5_STATE))
            s5p_im.append(hpi.reshape(B, S5_GROUPS, S5_STATE))
            s5s_re.append(hsr.reshape(NB, S5_GROUPS, S5_STATE))
            s5s_im.append(hsi.reshape(NB, S5_GROUPS, S5_STATE))
    return (xp.reshape(B, S, D_MODEL), xs.reshape(NB, tn, D_MODEL), jnp.stack(kv_p), jnp.stack(kv_s),
            jnp.stack(win_p), jnp.stack(win_s), jnp.stack(sgu_v_s), jnp.stack(s5p_re), jnp.stack(s5p_im),
            jnp.stack(s5s_re), jnp.stack(s5s_im))
```
